```python
import jax
import jax.numpy as jnp
from jax import lax
import numpy as np

D_MODEL = 2048
BATCH = 2
SEQ = 16384
DEPTH = 2

GRID_W = 64
CTX_LEN = 256
D_FF = 4 * D_MODEL
MIX_WIDTH = D_MODEL
A_WIDTH = MIX_WIDTH // 2
A_GROUPS = 8
A_CONV = 31
B_WIDTH = MIX_WIDTH // 2
B_HEADS = 8
B_BLOCK = B_WIDTH // B_HEADS
RG_CONV = 4
RG_C = 8.0
C_WIDTH = MIX_WIDTH // 2
C_GROUPS = 8
MLA_HEADS = 8
QK_NOPE = 128
QK_ROPE = 64
QK_DIM = QK_NOPE + QK_ROPE
V_DIM = 128
Q_RANK = 512
KV_RANK = 512
ROPE_BASE = 10000.0
ROPE_PAIRS_PER_AXIS = QK_ROPE // 4
Q_BLOCK = 128
EPS = 1e-6
N_AB = (DEPTH + 1) // 2
N_CD = DEPTH // 2

kernel_name = "hybrid_conv_rglru_fourier_mla_dit_block"


def rms_norm(x, g):
    xf = x.astype(jnp.float32)
    y = xf * lax.rsqrt(jnp.mean(xf * xf, axis=-1, keepdims=True) + EPS)
    return (y * g.astype(jnp.float32)).astype(x.dtype)


def modulation(cond, w, b):
    m = jax.nn.silu(cond) @ w + b
    return m.reshape(cond.shape[:-1] + (1, 6, D_MODEL))


def modulate(x, g, shift, scale):
    return rms_norm(x, g) * (1 + scale) + shift


def mlp_sublayer(x, m, g_pre, g_post, w1, w2):
    h = modulate(x, g_pre, m[..., 3, :], m[..., 4, :])
    y = jnp.square(jax.nn.relu(h @ w1)) @ w2
    return x + m[..., 5, :] * rms_norm(y, g_post)


def depthwise_conv(x, w, b, pad):
    y = lax.conv_general_dilated(
        x, w[:, None, :].astype(x.dtype), window_strides=(1,), padding=[pad],
        dimension_numbers=("NWC", "WIO", "NWC"), feature_group_count=x.shape[-1])
    return y + b


def conformer_conv(u, w, b, g, beta):
    y = depthwise_conv(u, w, b, (A_CONV // 2, A_CONV // 2))
    yg = y.reshape(y.shape[:2] + (A_GROUPS, -1)).astype(jnp.float32)
    mu = jnp.mean(yg, axis=-1, keepdims=True)
    d = yg - mu
    yn = (d * lax.rsqrt(jnp.mean(d * d, axis=-1, keepdims=True) + EPS)).reshape(y.shape)
    return jax.nn.silu(yn * g + beta).astype(u.dtype)


def rglru_coeffs(u, wa, ba, wi, bi, lam):
    ub = u.reshape(u.shape[:2] + (B_HEADS, B_BLOCK))
    r = jax.nn.sigmoid(jnp.einsum("blhi,hij->blhj", ub, wa).reshape(u.shape) + ba)
    i = jax.nn.sigmoid(jnp.einsum("blhi,hij->blhj", ub, wi).reshape(u.shape) + bi)
    log_a = RG_C * r.astype(jnp.float32) * jax.nn.log_sigmoid(lam.astype(jnp.float32))
    a = jnp.exp(log_a)
    bx = jnp.sqrt(-jnp.expm1(2.0 * log_a)) * (i * u).astype(jnp.float32)
    return a, bx


def _affine_combine(left, right):
    return left[0] * right[0], right[0] * left[1] + right[1]


def linear_scan(a, b, h0, reverse):
    if reverse:
        a, b = jnp.flip(a, 1), jnp.flip(b, 1)
    b = b.at[:, 0].add(a[:, 0] * h0)
    _, h = lax.associative_scan(_affine_combine, (a, b), axis=1)
    return jnp.flip(h, 1) if reverse else h


def bidirectional_rglru(xc, xl, conv_w, conv_b, wa, ba, wi, bi, lam):
    out_c, out_l = [], []
    for d, reverse in enumerate((False, True)):
        pad = (0, RG_CONV - 1) if reverse else (RG_CONV - 1, 0)

        def coeffs(u):
            return rglru_coeffs(depthwise_conv(u, conv_w[d], conv_b[d], pad), wa[d], ba[d], wi[d], bi[d], lam[d])

        a_c, b_c = coeffs(xc)
        h_c = linear_scan(a_c, b_c, jnp.zeros_like(b_c[:, 0]), reverse)
        a_l, b_l = coeffs(xl)
        h_l = linear_scan(a_l, b_l, h_c[:, 0] if reverse else h_c[:, -1], reverse)
        out_c.append(h_c)
        out_l.append(h_l)
    return (out_c[0] + out_c[1]).astype(xc.dtype), (out_l[0] + out_l[1]).astype(xl.dtype)


def conv_lru_mixer(hc, hl, w_in, w_out, cv_w, cv_b, cv_g, cv_beta,
                   rg_conv_w, rg_conv_b, rg_wa, rg_ba, rg_wi, rg_bi, rg_lambda, need_ctx):
    pl = jnp.split(hl @ w_in, 4, axis=-1)
    pc = jnp.split(hc @ w_in, 4, axis=-1)
    rec_c, rec_l = bidirectional_rglru(pc[2], pl[2], rg_conv_w, rg_conv_b, rg_wa, rg_ba, rg_wi, rg_bi, rg_lambda)

    def merge(p, rec):
        conv_out = conformer_conv(p[0] * jax.nn.sigmoid(p[1]), cv_w, cv_b, cv_g, cv_beta)
        return jnp.concatenate([conv_out, rec * jax.nn.gelu(p[3])], axis=-1) @ w_out

    return (merge(pc, rec_c) if need_ctx else None), merge(pl, rec_l)


def axial_rope(rows):
    row = jnp.repeat(jnp.arange(rows, dtype=jnp.float32), GRID_W)
    col = jnp.tile(jnp.arange(GRID_W, dtype=jnp.float32), rows)
    inv = ROPE_BASE ** (-jnp.arange(ROPE_PAIRS_PER_AXIS, dtype=jnp.float32) / ROPE_PAIRS_PER_AXIS)
    ang = jnp.concatenate([row[:, None] * inv, col[:, None] * inv], axis=-1)
    return jnp.cos(ang), jnp.sin(ang)


def apply_rope(x, cos, sin):
    x1, x2 = jnp.split(x.astype(jnp.float32), 2, axis=-1)
    return jnp.concatenate([x1 * cos - x2 * sin, x2 * cos + x1 * sin], axis=-1).astype(x.dtype)


def fourier_mix(f):
    fg = f.reshape(f.shape[:2] + (C_GROUPS, -1)).astype(jnp.float32)
    return jnp.fft.fftn(fg, axes=(1, 3), norm="ortho").real.reshape(f.shape).astype(f.dtype)


def attention(q, k, v):
    s = jnp.einsum("bqhd,bkhd->bhqk", q, k, preferred_element_type=jnp.float32) * (QK_DIM ** -0.5)
    p = jax.nn.softmax(s, axis=-1).astype(v.dtype)
    return jnp.einsum("bhqk,bkhd->bqhd", p, v)


def blocked_attention(q, k, v):
    bsz, n, h, dk = q.shape
    qb = jnp.moveaxis(q.reshape(bsz, n // Q_BLOCK, Q_BLOCK, h, dk), 1, 0)
    o = lax.map(lambda qi: attention(qi, k, v), qb)
    return jnp.moveaxis(o, 0, 1).reshape(bsz, n, h, -1)


def fourier_mla_mixer(hc, hl, cos, sin, w_in, w_out, q_g, kv_g, w_uq, w_ukv, need_ctx):
    cuts = [C_WIDTH, C_WIDTH + Q_RANK, C_WIDTH + Q_RANK + KV_RANK]
    fl, cql, ckvl, krl = jnp.split(hl @ w_in, cuts, axis=-1)
    fc, cqc, ckvc, krc = jnp.split(hc @ w_in, cuts, axis=-1)

    def queries(cq):
        return (rms_norm(cq, q_g) @ w_uq).reshape(cq.shape[:2] + (MLA_HEADS, QK_DIM))

    def keys_values(ckv, kr):
        kv = (rms_norm(ckv, kv_g) @ w_ukv).reshape(ckv.shape[:2] + (MLA_HEADS, QK_NOPE + V_DIM))
        k_nope, v = jnp.split(kv, [QK_NOPE], axis=-1)
        k_rope = jnp.broadcast_to(kr[:, :, None, :], k_nope.shape[:3] + (QK_ROPE,))
        return jnp.concatenate([k_nope, k_rope], axis=-1), v

    ql = queries(cql)
    ql = jnp.concatenate([ql[..., :QK_NOPE], apply_rope(ql[..., QK_NOPE:], cos[:, None, :], sin[:, None, :])], axis=-1)
    kl, vl = keys_values(ckvl, apply_rope(krl, cos, sin))
    kc, vc = keys_values(ckvc, krc)
    ol = blocked_attention(ql, jnp.concatenate([kc, kl], axis=1), jnp.concatenate([vc, vl], axis=1))
    yl = jnp.concatenate([fourier_mix(fl), ol.reshape(ol.shape[:2] + (-1,))], axis=-1) @ w_out
    if not need_ctx:
        return None, yl
    oc = attention(queries(cqc), kc, vc)
    yc = jnp.concatenate([fourier_mix(fc), oc.reshape(oc.shape[:2] + (-1,))], axis=-1) @ w_out
    return yc, yl


def setup_inputs(seed: int = 0) -> dict:
    key = jax.random.key(seed)
    ks = iter(jax.random.split(key, 32))

    def normal(shape, scale):
        return jax.random.normal(next(ks), shape, jnp.float32) * scale

    def gain(shape):
        return 1.0 + normal(shape, 0.05)

    D = D_MODEL
    u = jax.random.uniform(next(ks), (N_AB, 2, B_WIDTH), jnp.float32, 0.9, 0.999)
    s = u ** (1.0 / RG_C)
    rg_lambda = jnp.log(s) - jnp.log1p(-s)
    return {
        "x": normal((BATCH, SEQ, D), 1.0),
        "c": normal((BATCH, D), 1.0),
        "ctx": normal((BATCH, CTX_LEN, D), 1.0),
        "c_ctx": normal((D,), 1.0),
        "mod_w": normal((DEPTH, D, 6 * D), D ** -0.5),
        "mod_b": normal((DEPTH, 6 * D), 0.02),
        "norm_g": gain((DEPTH, 4, D)),
        "mlp_w1": normal((DEPTH, D, D_FF), D ** -0.5),
        "mlp_w2": normal((DEPTH, D_FF, D), D_FF ** -0.5),
        "ab_w_in": normal((N_AB, D, 2 * A_WIDTH + 2 * B_WIDTH), D ** -0.5),
        "ab_w_out": normal((N_AB, A_WIDTH + B_WIDTH, D), (A_WIDTH + B_WIDTH) ** -0.5),
        "cv_w": normal((N_AB, A_CONV, A_WIDTH), A_CONV ** -0.5),
        "cv_b": normal((N_AB, A_WIDTH), 0.02),
        "cv_norm_g": gain((N_AB, A_WIDTH)),
        "cv_norm_b": normal((N_AB, A_WIDTH), 0.02),
        "rg_conv_w": normal((N_AB, 2, RG_CONV, B_WIDTH), RG_CONV ** -0.5),
        "rg_conv_b": normal((N_AB, 2, B_WIDTH), 0.02),
        "rg_wa": normal((N_AB, 2, B_HEADS, B_BLOCK, B_BLOCK), B_BLOCK ** -0.5),
        "rg_ba": normal((N_AB, 2, B_WIDTH), 0.1),
        "rg_wi": normal((N_AB, 2, B_HEADS, B_BLOCK, B_BLOCK), B_BLOCK ** -0.5),
        "rg_bi": normal((N_AB, 2, B_WIDTH), 0.1),
        "rg_lambda": rg_lambda,
        "cd_w_in": normal((N_CD, D, C_WIDTH + Q_RANK + KV_RANK + QK_ROPE), D ** -0.5),
        "cd_w_out": normal((N_CD, C_WIDTH + MLA_HEADS * V_DIM, D), (C_WIDTH + MLA_HEADS * V_DIM) ** -0.5),
        "mla_q_norm_g": gain((N_CD, Q_RANK)),
        "mla_kv_norm_g": gain((N_CD, KV_RANK)),
        "mla_w_uq": normal((N_CD, Q_RANK, MLA_HEADS * QK_DIM), Q_RANK ** -0.5),
        "mla_w_ukv": normal((N_CD, KV_RANK, MLA_HEADS * (QK_NOPE + V_DIM)), KV_RANK ** -0.5),
    }


def reference(x, c, ctx, c_ctx, mod_w, mod_b, norm_g, mlp_w1, mlp_w2,
              ab_w_in, ab_w_out, cv_w, cv_b, cv_norm_g, cv_norm_b,
              rg_conv_w, rg_conv_b, rg_wa, rg_ba, rg_wi, rg_bi, rg_lambda,
              cd_w_in, cd_w_out, mla_q_norm_g, mla_kv_norm_g, mla_w_uq, mla_w_ukv):
    n_lat = x.shape[1]
    rows = n_lat // GRID_W
    cos, sin = axial_rope(rows)
    xc, xl = ctx, x
    for i in range(DEPTH):
        need_ctx = i < DEPTH - 1
        ml = modulation(c, mod_w[i], mod_b[i])
        mc = modulation(c_ctx, mod_w[i], mod_b[i])
        hl = modulate(xl, norm_g[i, 0], ml[..., 0, :], ml[..., 1, :])
        hc = modulate(xc, norm_g[i, 0], mc[..., 0, :], mc[..., 1, :])
        j = i // 2
        if i % 2 == 0:
            yc, yl = conv_lru_mixer(hc, hl, ab_w_in[j], ab_w_out[j], cv_w[j], cv_b[j], cv_norm_g[j], cv_norm_b[j],
                                    rg_conv_w[j], rg_conv_b[j], rg_wa[j], rg_ba[j], rg_wi[j], rg_bi[j],
                                    rg_lambda[j], need_ctx)
        else:
            yc, yl = fourier_mla_mixer(hc, hl, cos, sin, cd_w_in[j], cd_w_out[j], mla_q_norm_g[j],
                                       mla_kv_norm_g[j], mla_w_uq[j], mla_w_ukv[j], need_ctx)
        xl = xl + ml[..., 2, :] * rms_norm(yl, norm_g[i, 1])
        xl = mlp_sublayer(xl, ml, norm_g[i, 2], norm_g[i, 3], mlp_w1[i], mlp_w2[i])
        if need_ctx:
            xc = xc + mc[..., 2, :] * rms_norm(yc, norm_g[i, 1])
            xc = mlp_sublayer(xc, mc, norm_g[i, 2], norm_g[i, 3], mlp_w1[i], mlp_w2[i])
    return xl
```

```python
import functools
import math

import numpy as np
import jax
import jax.numpy as jnp
from jax import lax
from jax.experimental import pallas as pl
from jax.experimental.pallas import tpu as pltpu

F32 = jnp.float32
BF16 = jnp.bfloat16

EPS = 1e-6
GRID_W = 64
N_GROUPS = 8
GROUP = 128
A_CONV = 31
RG_CONV = 4
RG_C = 8.0
QK_NOPE = 128
QK_ROPE = 64
QK_DIM = QK_NOPE + QK_ROPE
V_DIM = 128
ROPE_BASE = 10000.0
QK_PAD = 256
CONV_HALO = 16
LRU_HALO = 8
VMEM_LIMIT = 56 * 1024 * 1024


def _params(sem, vmem=VMEM_LIMIT):
    return pltpu.CompilerParams(dimension_semantics=sem, vmem_limit_bytes=vmem)


def _pick(n, candidates):
    for c in candidates:
        if n % c == 0:
            return c
    raise ValueError(f"no tile for {n} in {candidates}")


def _rms(x, g):
    return x * lax.rsqrt(jnp.mean(x * x, axis=-1, keepdims=True) + EPS) * g


def _mod_kernel(c_ref, w_ref, b_ref, o_ref):
    c = c_ref[...]
    s = c * jax.nn.sigmoid(c)
    o_ref[0] = jnp.dot(s, w_ref[0], preferred_element_type=F32,
                       precision=lax.Precision.HIGHEST) + b_ref[0]


def _modulation(cond8, mod_w, mod_b):
    depth, d, n = mod_w.shape
    tn = _pick(n, (1024, 512, 256, 128))
    return pl.pallas_call(
        _mod_kernel,
        grid=(depth, n // tn),
        in_specs=[pl.BlockSpec((8, d), lambda l, j: (0, 0)),
                  pl.BlockSpec((1, d, tn), lambda l, j: (l, 0, j)),
                  pl.BlockSpec((1, 1, tn), lambda l, j: (l, 0, j))],
        out_specs=pl.BlockSpec((1, 8, tn), lambda l, j: (l, 0, j)),
        out_shape=jax.ShapeDtypeStruct((depth, 8, n), F32),
        name="modulation",
        compiler_params=_params(("arbitrary", "arbitrary")),
    )(cond8, mod_w, mod_b.reshape(depth, 1, n))


def _norm_matmul_kernel(*refs, shift_row, scale_row, aliased):
    if aliased:
        x_ref, m_ref, g_ref, w_ref, _, o_ref, h_ref = refs
    else:
        x_ref, m_ref, g_ref, w_ref, o_ref, h_ref = refs

    @pl.when(pl.program_id(2) == 0)
    def _():
        h = _rms(x_ref[0], g_ref[...])
        h = h * (1.0 + m_ref[0, scale_row:scale_row + 1, :]) + m_ref[0, shift_row:shift_row + 1, :]
        h_ref[...] = h.astype(BF16)

    o_ref[0] = jnp.dot(h_ref[...], w_ref[...], preferred_element_type=F32).astype(o_ref.dtype)


def _norm_matmul(x, mod, g, w, shift_row, scale_row, out_rows=None, row_block_off=0, into=None):
    b, l, d = x.shape
    n = w.shape[1]
    tm = _pick(l, (1024, 512, 256, 128))
    tn = _pick(n, (1024, 768, 512, 256, 128))
    out_rows = l if out_rows is None else out_rows
    off = row_block_off
    in_specs = [pl.BlockSpec((1, tm, d), lambda bi, i, j: (bi, i, 0)),
                pl.BlockSpec((1, 6, d), lambda bi, i, j: (bi, 0, 0)),
                pl.BlockSpec((1, d), lambda bi, i, j: (0, 0)),
                pl.BlockSpec((d, tn), lambda bi, i, j: (0, j))]
    args = [x, mod, g.reshape(1, d), w]
    aliases = {}
    if into is not None:
        in_specs.append(pl.BlockSpec(memory_space=pl.ANY))
        args.append(into)
        aliases = {4: 0}
    return pl.pallas_call(
        functools.partial(_norm_matmul_kernel, shift_row=shift_row, scale_row=scale_row,
                          aliased=into is not None),
        grid=(b, l // tm, n // tn),
        in_specs=in_specs,
        out_specs=pl.BlockSpec((1, tm, tn), lambda bi, i, j: (bi, i + off, j)),
        out_shape=jax.ShapeDtypeStruct((b, out_rows, n), F32),
        scratch_shapes=[pltpu.VMEM((tm, d), BF16)],
        input_output_aliases=aliases,
        name="norm_matmul",
        compiler_params=_params(("arbitrary", "arbitrary", "arbitrary")),
    )(*args)


def _conv_kernel(v_ref, gt_ref, vp_ref, gp_ref, vn_ref, gn_ref, w_ref, b_ref, lg_ref, lb_ref,
                 o_ref, ubuf, *, t, rc):
    i = pl.program_id(1)
    first = i == 0
    last = i == pl.num_programs(1) - 1
    halo = CONV_HALO
    ubuf[halo:halo + t, :] = v_ref[0] * jax.nn.sigmoid(gt_ref[0])
    up = vp_ref[0] * jax.nn.sigmoid(gp_ref[0])
    un = vn_ref[0] * jax.nn.sigmoid(gn_ref[0])
    ubuf[0:halo, :] = jnp.where(first, 0.0, up)
    ubuf[halo + t:2 * halo + t, :] = jnp.where(last, 0.0, un)
    base = halo - A_CONV // 2
    for r0 in range(0, t, rc):
        for g in range(N_GROUPS):
            cs = slice(g * GROUP, (g + 1) * GROUP)
            acc = jnp.zeros((rc, GROUP), F32)
            for k in range(A_CONV):
                acc = acc + w_ref[k:k + 1, cs] * ubuf[r0 + base + k:r0 + base + k + rc, cs]
            y = acc + b_ref[:, cs]
            mu = jnp.mean(y, axis=-1, keepdims=True)
            dlt = y - mu
            yn = dlt * lax.rsqrt(jnp.mean(dlt * dlt, axis=-1, keepdims=True) + EPS)
            z = yn * lg_ref[:, cs] + lb_ref[:, cs]
            o_ref[0, r0:r0 + rc, cs] = (z * jax.nn.sigmoid(z)).astype(o_ref.dtype)


def _conformer_conv(p, cv_w, cv_b, ln_g, ln_b):
    b, l, _ = p.shape
    c = cv_w.shape[1]
    t = _pick(l, (256, 128))
    rc = 64
    hb = t // CONV_HALO
    nhb = l // CONV_HALO
    main = lambda col: pl.BlockSpec((1, t, c), lambda bi, i: (bi, i, col))
    prev = lambda col: pl.BlockSpec((1, CONV_HALO, c), lambda bi, i: (bi, jnp.maximum(i * hb - 1, 0), col))
    nxt = lambda col: pl.BlockSpec((1, CONV_HALO, c), lambda bi, i: (bi, jnp.minimum((i + 1) * hb, nhb - 1), col))
    vec = pl.BlockSpec((1, c), lambda bi, i: (0, 0))
    return pl.pallas_call(
        functools.partial(_conv_kernel, t=t, rc=rc),
        grid=(b, l // t),
        in_specs=[main(0), main(1), prev(0), prev(1), nxt(0), nxt(1),
                  pl.BlockSpec((A_CONV, c), lambda bi, i: (0, 0)), vec, vec, vec],
        out_specs=pl.BlockSpec((1, t, c), lambda bi, i: (bi, i, 0)),
        out_shape=jax.ShapeDtypeStruct((b, l, c), BF16),
        scratch_shapes=[pltpu.VMEM((t + 2 * CONV_HALO, c), F32)],
        name="conformer_conv",
        compiler_params=_params(("arbitrary", "arbitrary")),
    )(p, p, p, p, p, p, cv_w, cv_b.reshape(1, c), ln_g.reshape(1, c), ln_b.reshape(1, c))


def _lru_kernel(x_ref, xh_ref, cw_ref, cb_ref, wg_ref, ba_ref, bi_ref, lam_ref, h0_ref,
                h_ref, hl_ref, xbuf, a_buf, b_buf, carry, *, t, reverse):
    i = pl.program_id(1)
    c = x_ref.shape[-1]
    halo = LRU_HALO

    @pl.when(i == 0)
    def _():
        carry[...] = h0_ref[0]

    edge = jnp.where(i == 0, 0.0, xh_ref[0])
    if reverse:
        xbuf[0:t, :] = x_ref[0]
        xbuf[t:t + halo, :] = edge
        base = 0
    else:
        xbuf[0:halo, :] = edge
        xbuf[halo:halo + t, :] = x_ref[0]
        base = halo - (RG_CONV - 1)
    y = jnp.zeros((t, c), F32) + cb_ref[...]
    for k in range(RG_CONV):
        y = y + cw_ref[k:k + 1, :] * xbuf[base + k:base + k + t, :]

    log_sig = jax.nn.log_sigmoid(lam_ref[...])
    for h in range(N_GROUPS):
        cs = slice(h * GROUP, (h + 1) * GROUP)
        yh = y[:, cs]
        gates = jnp.dot(yh.astype(BF16), wg_ref[h], preferred_element_type=F32)
        r = jax.nn.sigmoid(gates[:, :GROUP] + ba_ref[:, cs])
        ig = jax.nn.sigmoid(gates[:, GROUP:] + bi_ref[:, cs])
        log_a = RG_C * r * log_sig[:, cs]
        a = jnp.exp(log_a)
        a_buf[:, cs] = a
        b_buf[:, cs] = jnp.sqrt(1.0 - a * a) * (ig * yh)

    ngroups = t // 8
    row = lax.broadcasted_iota(jnp.int32, (8, c), 0)

    def body(gidx, cr):
        gi = (ngroups - 1 - gidx) if reverse else gidx
        r0 = pl.multiple_of(gi * 8, 8)
        a = a_buf[pl.ds(r0, 8), :]
        bb = b_buf[pl.ds(r0, 8), :]
        for k in (1, 2, 4):
            if reverse:
                a_s = pltpu.roll(a, 8 - k, 0)
                b_s = pltpu.roll(bb, 8 - k, 0)
                valid = row < 8 - k
            else:
                a_s = pltpu.roll(a, k, 0)
                b_s = pltpu.roll(bb, k, 0)
                valid = row >= k
            bb = jnp.where(valid, a * b_s + bb, bb)
            a = jnp.where(valid, a * a_s, a)
        hh = bb + a * cr
        h_ref[0, pl.ds(r0, 8), :] = hh
        edge_row = hh[0:1, :] if reverse else hh[7:8, :]
        return jnp.broadcast_to(edge_row, (8, c))

    cr = lax.fori_loop(0, ngroups, body, carry[...])
    carry[...] = cr
    hl_ref[0] = cr


def _rglru(p, col, h0, conv_w, conv_b, wa, ba, wi, bi, lam, reverse):
    b, l, _ = p.shape
    c = conv_w.shape[1]
    t = _pick(l, (256, 128))
    nt = l // t
    hb = t // LRU_HALO
    nhb = l // LRU_HALO
    if reverse:
        tile = lambda bi_, i: (bi_, nt - 1 - i, col)
        halo = lambda bi_, i: (bi_, jnp.minimum((nt - i) * hb, nhb - 1), col)
        otile = lambda bi_, i: (bi_, nt - 1 - i, 0)
    else:
        tile = lambda bi_, i: (bi_, i, col)
        halo = lambda bi_, i: (bi_, jnp.maximum(i * hb - 1, 0), col)
        otile = lambda bi_, i: (bi_, i, 0)
    wg = jnp.concatenate([wa, wi], axis=-1).astype(BF16)
    vec = pl.BlockSpec((1, c), lambda bi_, i: (0, 0))
    state = pl.BlockSpec((1, 8, c), lambda bi_, i: (bi_, 0, 0))
    return pl.pallas_call(
        functools.partial(_lru_kernel, t=t, reverse=reverse),
        grid=(b, nt),
        in_specs=[pl.BlockSpec((1, t, c), tile),
                  pl.BlockSpec((1, LRU_HALO, c), halo),
                  pl.BlockSpec((RG_CONV, c), lambda bi_, i: (0, 0)), vec,
                  pl.BlockSpec((N_GROUPS, GROUP, 2 * GROUP), lambda bi_, i: (0, 0, 0)),
                  vec, vec, vec, state],
        out_specs=[pl.BlockSpec((1, t, c), otile), state],
        out_shape=[jax.ShapeDtypeStruct((b, l, c), F32), jax.ShapeDtypeStruct((b, 8, c), F32)],
        scratch_shapes=[pltpu.VMEM((t + LRU_HALO, c), F32), pltpu.VMEM((t, c), F32),
                        pltpu.VMEM((t, c), F32), pltpu.VMEM((8, c), F32)],
        name="rglru_rev" if reverse else "rglru_fwd",
        compiler_params=_params(("arbitrary", "arbitrary")),
    )(p, p, conv_w, conv_b.reshape(1, c), wg, ba.reshape(1, c), bi.reshape(1, c), lam.reshape(1, c), h0)


def _mixout_ab_kernel(cv_ref, hf_ref, hr_ref, pg_ref, x_ref, m_ref, g_ref, wa_ref, wb_ref, o_ref):
    rec = (hf_ref[0] + hr_ref[0]) * jax.nn.gelu(pg_ref[0])
    y = jnp.dot(cv_ref[0], wa_ref[...], preferred_element_type=F32)
    y = y + jnp.dot(rec.astype(BF16), wb_ref[...], preferred_element_type=F32)
    o_ref[0] = x_ref[0] + m_ref[0, 2:3, :] * _rms(y, g_ref[...])


def _mixout_ab(conv_out, h_f, h_r, p, x, mod, g, w_out):
    b, l, d = x.shape
    c = conv_out.shape[-1]
    tm = _pick(l, (512, 256, 128))
    half = lambda: pl.BlockSpec((1, tm, c), lambda bi, i: (bi, i, 0))
    wspec = lambda k: pl.BlockSpec((c, d), lambda bi, i: (k, 0))
    return pl.pallas_call(
        _mixout_ab_kernel,
        grid=(b, l // tm),
        in_specs=[half(), half(), half(),
                  pl.BlockSpec((1, tm, c), lambda bi, i: (bi, i, 3)),
                  pl.BlockSpec((1, tm, d), lambda bi, i: (bi, i, 0)),
                  pl.BlockSpec((1, 6, d), lambda bi, i: (bi, 0, 0)),
                  pl.BlockSpec((1, d), lambda bi, i: (0, 0)),
                  wspec(0), wspec(1)],
        out_specs=pl.BlockSpec((1, tm, d), lambda bi, i: (bi, i, 0)),
        out_shape=jax.ShapeDtypeStruct((b, l, d), F32),
        name="mixout_ab",
        compiler_params=_params(("arbitrary", "arbitrary")),
    )(conv_out, h_f, h_r, p, x, mod, g.reshape(1, d), w_out, w_out)


def _mixout_cd_kernel(f_ref, o_att_ref, x_ref, m_ref, g_ref, wa_ref, wb_ref, o_ref):
    y = jnp.dot(f_ref[0].astype(BF16), wa_ref[...], preferred_element_type=F32)
    y = y + jnp.dot(o_att_ref[0], wb_ref[...], preferred_element_type=F32)
    o_ref[0] = x_ref[0] + m_ref[0, 2:3, :] * _rms(y, g_ref[...])


def _mixout_cd(four, att, x, mod, g, w_out):
    b, l, d = x.shape
    c = four.shape[-1]
    tm = _pick(l, (512, 256, 128))
    half = lambda: pl.BlockSpec((1, tm, c), lambda bi, i: (bi, i, 0))
    wspec = lambda k: pl.BlockSpec((c, d), lambda bi, i: (k, 0))
    return pl.pallas_call(
        _mixout_cd_kernel,
        grid=(b, l // tm),
        in_specs=[half(), half(),
                  pl.BlockSpec((1, tm, d), lambda bi, i: (bi, i, 0)),
                  pl.BlockSpec((1, 6, d), lambda bi, i: (bi, 0, 0)),
                  pl.BlockSpec((1, d), lambda bi, i: (0, 0)),
                  wspec(0), wspec(1)],
        out_specs=pl.BlockSpec((1, tm, d), lambda bi, i: (bi, i, 0)),
        out_shape=jax.ShapeDtypeStruct((b, l, d), F32),
        name="mixout_cd",
        compiler_params=_params(("arbitrary", "arbitrary")),
    )(four, att, x, mod, g.reshape(1, d), w_out, w_out)


def _mlp_kernel(x_ref, m_ref, gpre_ref, gpost_ref, w1_ref, w2_ref, o_ref, h_ref, acc_ref):
    j = pl.program_id(2)

    @pl.when(j == 0)
    def _():
        h = _rms(x_ref[0], gpre_ref[...])
        h = h * (1.0 + m_ref[0, 4:5, :]) + m_ref[0, 3:4, :]
        h_ref[...] = h.astype(BF16)
        acc_ref[...] = jnp.zeros_like(acc_ref)

    u = jnp.dot(h_ref[...], w1_ref[...], preferred_element_type=F32)
    u = jnp.square(jnp.maximum(u, 0.0))
    acc_ref[...] += jnp.dot(u.astype(BF16), w2_ref[...], preferred_element_type=F32)

    @pl.when(j == pl.num_programs(2) - 1)
    def _():
        o_ref[0] = x_ref[0] + m_ref[0, 5:6, :] * _rms(acc_ref[...], gpost_ref[...])


def _mlp(x, mod, g_pre, g_post, w1, w2):
    b, l, d = x.shape
    f = w1.shape[1]
    tm = _pick(l, (512, 256, 128))
    tf = _pick(f, (512, 256, 128))
    return pl.pallas_call(
        _mlp_kernel,
        grid=(b, l // tm, f // tf),
        in_specs=[pl.BlockSpec((1, tm, d), lambda bi, i, j: (bi, i, 0)),
                  pl.BlockSpec((1, 6, d), lambda bi, i, j: (bi, 0, 0)),
                  pl.BlockSpec((1, d), lambda bi, i, j: (0, 0)),
                  pl.BlockSpec((1, d), lambda bi, i, j: (0, 0)),
                  pl.BlockSpec((d, tf), lambda bi, i, j: (0, j)),
                  pl.BlockSpec((tf, d), lambda bi, i, j: (j, 0))],
        out_specs=pl.BlockSpec((1, tm, d), lambda bi, i, j: (bi, i, 0)),
        out_shape=jax.ShapeDtypeStruct((b, l, d), F32),
        scratch_shapes=[pltpu.VMEM((tm, d), BF16), pltpu.VMEM((tm, d), F32)],
        name="mlp",
        compiler_params=_params(("arbitrary", "arbitrary", "arbitrary")),
    )(x, mod, g_pre.reshape(1, d), g_post.reshape(1, d), w1, w2)


def _fourier_tables(l):
    n1 = 1 << (int(math.log2(l)) // 2)
    while l % n1:
        n1 //= 2
    n2 = l // n1
    def cs(n):
        idx = np.arange(n)
        ang = 2.0 * np.pi * ((idx[:, None] * idx[None, :]) % n) / n
        return np.cos(ang), np.sin(ang)
    c1, s1 = cs(n1)
    c2, s2 = cs(n2)
    cc, sc = cs(GROUP)
    delta = 2.0 * np.pi * np.arange(n1)[:, None] / l * np.ones((1, GROUP))
    norm = 1.0 / math.sqrt(l * GROUP)
    chan = np.concatenate([cc, -sc], axis=1)
    f1 = np.concatenate([c1, s1], axis=0)
    f2 = np.concatenate([c2, s2], axis=1) * norm
    return (n1, n2, jnp.asarray(chan, BF16), jnp.asarray(f1, BF16), jnp.asarray(f2, BF16),
            jnp.asarray(np.cos(delta), F32), jnp.asarray(np.sin(delta), F32))


def _fourier_kernel(x_ref, chan_ref, f1_ref, f2_ref, dc_ref, ds_ref, o_ref, are_buf, aim_buf, tw_c, tw_s,
                    *, n1, n2):
    tw_c[...] = jnp.ones_like(tw_c)
    tw_s[...] = jnp.zeros_like(tw_s)

    def stage1(j, _):
        xs = x_ref[0, pl.ds(j, n1, stride=n2), :]
        pq = jnp.dot(xs.astype(BF16), chan_ref[...], preferred_element_type=F32)
        r = jnp.dot(f1_ref[...], pq.astype(BF16), preferred_element_type=F32)
        a_re = r[:n1, :GROUP] + r[n1:, GROUP:]
        a_im = r[:n1, GROUP:] - r[n1:, :GROUP]
        tc = tw_c[...]
        ts = tw_s[...]
        row0 = pl.multiple_of(j * n1, 8)
        are_buf[pl.ds(row0, n1), :] = a_re * tc + a_im * ts
        aim_buf[pl.ds(row0, n1), :] = a_im * tc - a_re * ts
        tw_c[...] = tc * dc_ref[...] - ts * ds_ref[...]
        tw_s[...] = ts * dc_ref[...] + tc * ds_ref[...]
        return 0

    lax.fori_loop(0, n2, stage1, 0)

    def stage2(k1, _):
        re = are_buf[pl.ds(k1, n2, stride=n1), :]
        im = aim_buf[pl.ds(k1, n2, stride=n1), :]
        rhs = jnp.concatenate([re, im], axis=0).astype(BF16)
        o_ref[0, pl.ds(k1, n2, stride=n1), :] = jnp.dot(f2_ref[...], rhs, preferred_element_type=F32)
        return 0

    lax.fori_loop(0, n1, stage2, 0)


def _fourier_mix(p, l):
    b = p.shape[0]
    n1, n2, chan, f1, f2, dc, ds = _fourier_tables(l)
    const = lambda a: pl.BlockSpec(a.shape, lambda bi, g: (0, 0))
    return pl.pallas_call(
        functools.partial(_fourier_kernel, n1=n1, n2=n2),
        grid=(b, N_GROUPS),
        in_specs=[pl.BlockSpec((1, l, GROUP), lambda bi, g: (bi, 0, g)),
                  const(chan), const(f1), const(f2), const(dc), const(ds)],
        out_specs=pl.BlockSpec((1, l, GROUP), lambda bi, g: (bi, 0, g)),
        out_shape=jax.ShapeDtypeStruct((b, l, N_GROUPS * GROUP), F32),
        scratch_shapes=[pltpu.VMEM((l, GROUP), F32), pltpu.VMEM((l, GROUP), F32),
                        pltpu.VMEM((n1, GROUP), F32), pltpu.VMEM((n1, GROUP), F32)],
        name="fourier_mix",
        compiler_params=_params(("arbitrary", "arbitrary"), 62 * 1024 * 1024),
    )(p, chan, f1, f2, dc, ds)


def _rope_tables(l, n_ctx):
    t = np.arange(l)
    inv = ROPE_BASE ** (-np.arange(QK_ROPE // 4, dtype=np.float64) / (QK_ROPE // 4))
    ang = np.concatenate([(t // GRID_W)[:, None] * inv, (t % GRID_W)[:, None] * inv], axis=-1)
    ang = np.concatenate([ang, np.zeros((n_ctx, QK_ROPE // 2))], axis=0)
    cos, sin = np.cos(ang), np.sin(ang)
    return jnp.asarray(np.concatenate([cos, cos, -sin, sin], axis=-1), F32)


def _rope_apply(r, cs):
    tt = r * cs
    ro = tt + pltpu.roll(tt, QK_ROPE, 1)
    lane = lax.broadcasted_iota(jnp.int32, ro.shape, 1)
    return jnp.where(lane < QK_ROPE, ro, 0.0)


def _q_kernel(cq_ref, g_ref, w_ref, cs_ref, o_ref, n_ref):
    @pl.when(pl.program_id(2) == 0)
    def _():
        n_ref[...] = _rms(cq_ref[0], g_ref[...]).astype(BF16)

    res = jnp.dot(n_ref[...], w_ref[0], preferred_element_type=F32)
    q = jnp.concatenate([res[:, :QK_NOPE], _rope_apply(res[:, QK_NOPE:], cs_ref[...])], axis=1)
    o_ref[0, 0] = (q * (QK_DIM ** -0.5)).astype(o_ref.dtype)


def _mla_queries(p, l, col, q_g, w_uq, cs):
    b = p.shape[0]
    rank = w_uq.shape[0]
    w = w_uq.reshape(rank, N_GROUPS, QK_DIM)
    x1 = w[:, :, QK_NOPE:QK_NOPE + QK_ROPE // 2]
    x2 = w[:, :, QK_NOPE + QK_ROPE // 2:]
    w = jnp.concatenate([w[:, :, :QK_NOPE], x1, x2, x2, x1], axis=-1)
    w = jnp.transpose(w, (1, 0, 2)).astype(BF16)
    tm = _pick(l, (1024, 512, 256, 128))
    return pl.pallas_call(
        _q_kernel,
        grid=(b, l // tm, N_GROUPS),
        in_specs=[pl.BlockSpec((1, tm, rank), lambda bi, i, h: (bi, i, col)),
                  pl.BlockSpec((1, rank), lambda bi, i, h: (0, 0)),
                  pl.BlockSpec((1, rank, QK_PAD), lambda bi, i, h: (h, 0, 0)),
                  pl.BlockSpec((tm, 128), lambda bi, i, h: (i, 0))],
        out_specs=pl.BlockSpec((1, 1, tm, QK_PAD), lambda bi, i, h: (bi, h, i, 0)),
        out_shape=jax.ShapeDtypeStruct((b, N_GROUPS, l, QK_PAD), BF16),
        scratch_shapes=[pltpu.VMEM((tm, rank), BF16)],
        name="mla_q",
        compiler_params=_params(("arbitrary", "arbitrary", "arbitrary")),
    )(p, q_g.reshape(1, rank), w, cs)


def _kv_kernel(ckv_ref, kr_ref, g_ref, w_ref, cs_ref, k_ref, v_ref, n_ref, rope_ref):
    @pl.when(pl.program_id(2) == 0)
    def _():
        n_ref[...] = _rms(ckv_ref[0], g_ref[...]).astype(BF16)
        rope_ref[...] = _rope_apply(kr_ref[0], cs_ref[...]).astype(BF16)

    res = jnp.dot(n_ref[...], w_ref[...], preferred_element_type=F32)
    k_ref[0, 0] = jnp.concatenate([res[:, :QK_NOPE].astype(BF16), rope_ref[...]], axis=1)
    v_ref[0, 0] = res[:, QK_NOPE:].astype(BF16)


def _mla_keys_values(p, col_kv, col_kr, kv_g, w_ukv, cs):
    b, lk, _ = p.shape
    rank = w_ukv.shape[0]
    tm = _pick(lk, (1280, 640, 256, 128))
    hw = QK_NOPE + V_DIM
    return pl.pallas_call(
        _kv_kernel,
        grid=(b, lk // tm, N_GROUPS),
        in_specs=[pl.BlockSpec((1, tm, rank), lambda bi, i, h: (bi, i, col_kv)),
                  pl.BlockSpec((1, tm, 128), lambda bi, i, h: (bi, i, col_kr)),
                  pl.BlockSpec((1, rank), lambda bi, i, h: (0, 0)),
                  pl.BlockSpec((rank, hw), lambda bi, i, h: (0, h)),
                  pl.BlockSpec((tm, 128), lambda bi, i, h: (i, 0))],
        out_specs=[pl.BlockSpec((1, 1, tm, QK_PAD), lambda bi, i, h: (bi, h, i, 0)),
                   pl.BlockSpec((1, 1, tm, V_DIM), lambda bi, i, h: (bi, h, i, 0))],
        out_shape=[jax.ShapeDtypeStruct((b, N_GROUPS, lk, QK_PAD), BF16),
                   jax.ShapeDtypeStruct((b, N_GROUPS, lk, V_DIM), BF16)],
        scratch_shapes=[pltpu.VMEM((tm, rank), BF16), pltpu.VMEM((tm, 128), BF16)],
        name="mla_kv",
        compiler_params=_params(("arbitrary", "arbitrary", "arbitrary")),
    )(p, p, kv_g.reshape(1, rank), w_ukv.astype(BF16), cs)


def _attn_kernel(q_ref, k_ref, v_ref, o_ref, m_ref, l_ref, acc_ref, *, tk):
    j = pl.program_id(3)

    @pl.when(j == 0)
    def _():
        m_ref[...] = jnp.full_like(m_ref, -jnp.inf)
        l_ref[...] = jnp.zeros_like(l_ref)
        acc_ref[...] = jnp.zeros_like(acc_ref)

    s = lax.dot_general(q_ref[0, 0], k_ref[0, 0], (((1,), (1,)), ((), ())),
                        preferred_element_type=F32)
    m_prev = m_ref[...]
    m_next = jnp.maximum(m_prev, jnp.max(s, axis=1, keepdims=True))
    p = jnp.exp(s - pltpu.repeat(m_next, tk // 128, axis=1))
    alpha = jnp.exp(m_prev - m_next)
    l_ref[...] = alpha * l_ref[...] + jnp.sum(p, axis=1, keepdims=True)
    acc_ref[...] = alpha * acc_ref[...] + jnp.dot(p.astype(BF16), v_ref[0, 0],
                                                  preferred_element_type=F32)
    m_ref[...] = m_next

    @pl.when(j == pl.num_programs(3) - 1)
    def _():
        o_ref[0] = (acc_ref[...] / l_ref[...]).astype(o_ref.dtype)


def _attention(q, k, v):
    b, h, l, _ = q.shape
    lk = k.shape[2]
    tq = _pick(l, (1024, 512, 256, 128))
    tk = _pick(lk, (1280, 640, 256, 128))
    return pl.pallas_call(
        functools.partial(_attn_kernel, tk=tk),
        grid=(b, h, l // tq, lk // tk),
        in_specs=[pl.BlockSpec((1, 1, tq, QK_PAD), lambda bi, hi, i, j: (bi, hi, i, 0)),
                  pl.BlockSpec((1, 1, tk, QK_PAD), lambda bi, hi, i, j: (bi, hi, j, 0)),
                  pl.BlockSpec((1, 1, tk, V_DIM), lambda bi, hi, i, j: (bi, hi, j, 0))],
        out_specs=pl.BlockSpec((1, tq, V_DIM), lambda bi, hi, i, j: (bi, i, hi)),
        out_shape=jax.ShapeDtypeStruct((b, l, h * V_DIM), BF16),
        scratch_shapes=[pltpu.VMEM((tq, 128), F32), pltpu.VMEM((tq, 128), F32),
                        pltpu.VMEM((tq, V_DIM), F32)],
        name="flash_attention",
        compiler_params=_params(("arbitrary", "arbitrary", "arbitrary", "arbitrary")),
    )(q, k, v)


def _conv_lru_layer(xc, xl, ml, mc, ng, w1, w2, w_in, w_out, cv_w, cv_b, cv_g, cv_beta,
                    rg_conv_w, rg_conv_b, rg_wa, rg_ba, rg_wi, rg_bi, rg_lambda, need_ctx):
    w_in = w_in.astype(BF16)
    w_out = w_out.astype(BF16)
    w1 = w1.astype(BF16)
    w2 = w2.astype(BF16)
    pc = _norm_matmul(xc, mc, ng[0], w_in, 0, 1)
    plat = _norm_matmul(xl, ml, ng[0], w_in, 0, 1)
    c = cv_w.shape[1]
    zero = jnp.zeros((xl.shape[0], 8, c), F32)
    hs_c, hs_l = [], []
    for d, reverse in enumerate((False, True)):
        args = (rg_conv_w[d], rg_conv_b[d], rg_wa[d], rg_ba[d], rg_wi[d], rg_bi[d], rg_lambda[d], reverse)
        h_c, last_c = _rglru(pc, 2, zero, *args)
        h_l, _ = _rglru(plat, 2, last_c, *args)
        hs_c.append(h_c)
        hs_l.append(h_l)

    def finish(p, hs, x, mod):
        conv_out = _conformer_conv(p, cv_w, cv_b, cv_g, cv_beta)
        x = _mixout_ab(conv_out, hs[0], hs[1], p, x, mod, ng[1], w_out)
        return _mlp(x, mod, ng[2], ng[3], w1, w2)

    xl = finish(plat, hs_l, xl, ml)
    if need_ctx:
        xc = finish(pc, hs_c, xc, mc)
    return xc, xl


def _fourier_mla_layer(xc, xl, ml, mc, ng, w1, w2, w_in, w_out, q_g, kv_g, w_uq, w_ukv):
    b, l, d = xl.shape
    n_ctx = xc.shape[1]
    c = N_GROUPS * GROUP
    q_rank = q_g.shape[0]
    kv_rank = kv_g.shape[0]
    kr = w_in[:, c + q_rank + kv_rank:]
    kr1, kr2 = kr[:, :QK_ROPE // 2], kr[:, QK_ROPE // 2:]
    n_used = c + q_rank + kv_rank + 2 * QK_ROPE
    n_pad = -n_used % 256
    w_in_p = jnp.concatenate([w_in[:, :c + q_rank + kv_rank], kr1, kr2, kr2, kr1,
                              jnp.zeros((d, n_pad), w_in.dtype)], axis=1).astype(BF16)
    lk = l + n_ctx
    p = _norm_matmul(xl, ml, ng[0], w_in_p, 0, 1, out_rows=lk)
    tmc = _pick(n_ctx, (1024, 512, 256, 128))
    p = _norm_matmul(xc, mc, ng[0], w_in_p, 0, 1, out_rows=lk, row_block_off=l // tmc, into=p)
    cs = _rope_tables(l, n_ctx)
    four = _fourier_mix(p, l)
    q = _mla_queries(p, l, c // q_rank, q_g, w_uq, cs)
    k, v = _mla_keys_values(p, (c + q_rank) // kv_rank, (c + q_rank + kv_rank) // 128, kv_g, w_ukv, cs)
    att = _attention(q, k, v)
    xl = _mixout_cd(four, att, xl, ml, ng[1], w_out.astype(BF16))
    return _mlp(xl, ml, ng[2], ng[3], w1.astype(BF16), w2.astype(BF16))


def kernel(x, c, ctx, c_ctx, mod_w, mod_b, norm_g, mlp_w1, mlp_w2, ab_w_in, ab_w_out, cv_w, cv_b, cv_norm_g, cv_norm_b, rg_conv_w, rg_conv_b, rg_wa, rg_ba, rg_wi, rg_bi, rg_lambda, cd_w_in, cd_w_out, mla_q_norm_g, mla_kv_norm_g, mla_w_uq, mla_w_ukv):
    b, l, d = x.shape
    depth = mod_w.shape[0]
    assert b + 1 <= 8 and l % GRID_W == 0
    cond8 = jnp.concatenate([c, c_ctx[None, :], jnp.zeros((8 - b - 1, d), F32)], axis=0)
    mods = _modulation(cond8, mod_w, mod_b)
    xc, xl = ctx, x
    for i in range(depth):
        need_ctx = i < depth - 1
        ml = mods[i, :b].reshape(b, 6, d)
        mc = jnp.broadcast_to(mods[i, b].reshape(1, 6, d), (b, 6, d))
        j = i // 2
        if i % 2 == 0:
            xc, xl = _conv_lru_layer(xc, xl, ml, mc, norm_g[i], mlp_w1[i], mlp_w2[i], ab_w_in[j], ab_w_out[j],
                                     cv_w[j], cv_b[j], cv_norm_g[j], cv_norm_b[j], rg_conv_w[j], rg_conv_b[j],
                                     rg_wa[j], rg_ba[j], rg_wi[j], rg_bi[j], rg_lambda[j], need_ctx)
        else:
            xl = _fourier_mla_layer(xc, xl, ml, mc, norm_g[i], mlp_w1[i], mlp_w2[i], cd_w_in[j], cd_w_out[j],
                                    mla_q_norm_g[j], mla_kv_norm_g[j], mla_w_uq[j], mla_w_ukv[j])
    return xl
```

```python
import functools
import math

import numpy as np
import jax
import jax.numpy as jnp
from jax import lax
from jax.experimental import pallas as pl
from jax.experimental.pallas import tpu as pltpu

F32 = jnp.float32
BF16 = jnp.bfloat16

EPS = 1e-6
GRID_W = 64
N_GROUPS = 8
GROUP = 128
A_CONV = 31
RG_CONV = 4
RG_C = 8.0
QK_NOPE = 128
QK_ROPE = 64
QK_DIM = QK_NOPE + QK_ROPE
V_DIM = 128
ROPE_BASE = 10000.0
QK_PAD = 256
CONV_HALO = 16
LRU_HALO = 8
VMEM_LIMIT = 56 * 1024 * 1024


def _params(sem, vmem=VMEM_LIMIT):
    return pltpu.CompilerParams(dimension_semantics=sem, vmem_limit_bytes=vmem)


def _pick(n, candidates):
    for c in candidates:
        if n % c == 0:
            return c
    raise ValueError(f"no tile for {n} in {candidates}")


def _rms(x, g):
    return x * lax.rsqrt(jnp.mean(x * x, axis=-1, keepdims=True) + EPS) * g


ROW_CHUNK = 16


def _for_row_chunks(n_rows, body, unroll):
    def step(i, _):
        body(pl.multiple_of(i * ROW_CHUNK, ROW_CHUNK))
        return 0
    lax.fori_loop(0, n_rows // ROW_CHUNK, step, 0, unroll=unroll)


def _row_rsqrt_ms(src, r_ref):
    def body(r0):
        rows = pl.ds(r0, ROW_CHUNK)
        x = src(rows)
        r = lax.rsqrt(jnp.mean(x * x, axis=-1, keepdims=True) + EPS)
        r_ref[rows, :] = jnp.broadcast_to(r, (ROW_CHUNK, r_ref.shape[1]))

    _for_row_chunks(r_ref.shape[0], body, 8)


def _norm_modulate_rows(x_ref, g_ref, m_ref, shift_row, scale_row, h_ref, r_ref):
    _row_rsqrt_ms(lambda rows: x_ref[0, rows, :], r_ref)
    gain = g_ref[...] * (1.0 + m_ref[0, scale_row:scale_row + 1, :])
    shift = m_ref[0, shift_row:shift_row + 1, :]
    reps = h_ref.shape[1] // r_ref.shape[1]

    def body(r0):
        rows = pl.ds(r0, ROW_CHUNK)
        r = pltpu.repeat(r_ref[rows, :], reps, axis=1)
        h_ref[rows, :] = (x_ref[0, rows, :] * r * gain + shift).astype(h_ref.dtype)

    _for_row_chunks(h_ref.shape[0], body, 4)


def _mod_kernel(c_ref, w_ref, b_ref, o_ref):
    c = c_ref[...]
    s = c * jax.nn.sigmoid(c)
    o_ref[0] = jnp.dot(s, w_ref[0], preferred_element_type=F32,
                       precision=lax.Precision.HIGHEST) + b_ref[0]


def _modulation(cond8, mod_w, mod_b):
    depth, d, n = mod_w.shape
    tn = _pick(n, (1024, 512, 256, 128))
    return pl.pallas_call(
        _mod_kernel,
        grid=(depth, n // tn),
        in_specs=[pl.BlockSpec((8, d), lambda l, j: (0, 0)),
                  pl.BlockSpec((1, d, tn), lambda l, j: (l, 0, j)),
                  pl.BlockSpec((1, 1, tn), lambda l, j: (l, 0, j))],
        out_specs=pl.BlockSpec((1, 8, tn), lambda l, j: (l, 0, j)),
        out_shape=jax.ShapeDtypeStruct((depth, 8, n), F32),
        name="modulation",
        compiler_params=_params(("arbitrary", "arbitrary")),
    )(cond8, mod_w, mod_b.reshape(depth, 1, n))


def _norm_matmul_kernel(*refs, shift_row, scale_row, aliased):
    if aliased:
        x_ref, m_ref, g_ref, w_ref, _, o_ref, h_ref, r_ref = refs
    else:
        x_ref, m_ref, g_ref, w_ref, o_ref, h_ref, r_ref = refs

    @pl.when(pl.program_id(2) == 0)
    def _():
        _norm_modulate_rows(x_ref, g_ref, m_ref, shift_row, scale_row, h_ref, r_ref)

    o_ref[0] = jnp.dot(h_ref[...], w_ref[...], preferred_element_type=F32).astype(o_ref.dtype)


def _norm_matmul(x, mod, g, w, shift_row, scale_row, out_rows=None, row_block_off=0, into=None):
    b, l, d = x.shape
    n = w.shape[1]
    tm = _pick(l, (1024, 512, 256, 128))
    tn = _pick(n, (1024, 768, 512, 256, 128))
    out_rows = l if out_rows is None else out_rows
    off = row_block_off
    in_specs = [pl.BlockSpec((1, tm, d), lambda bi, i, j: (bi, i, 0)),
                pl.BlockSpec((1, 6, d), lambda bi, i, j: (bi, 0, 0)),
                pl.BlockSpec((1, d), lambda bi, i, j: (0, 0)),
                pl.BlockSpec((d, tn), lambda bi, i, j: (0, j))]
    args = [x, mod, g.reshape(1, d), w]
    aliases = {}
    if into is not None:
        in_specs.append(pl.BlockSpec(memory_space=pl.ANY))
        args.append(into)
        aliases = {4: 0}
    return pl.pallas_call(
        functools.partial(_norm_matmul_kernel, shift_row=shift_row, scale_row=scale_row,
                          aliased=into is not None),
        grid=(b, l // tm, n // tn),
        in_specs=in_specs,
        out_specs=pl.BlockSpec((1, tm, tn), lambda bi, i, j: (bi, i + off, j)),
        out_shape=jax.ShapeDtypeStruct((b, out_rows, n), F32),
        scratch_shapes=[pltpu.VMEM((tm, d), BF16), pltpu.VMEM((tm, 128), F32)],
        input_output_aliases=aliases,
        name="norm_matmul",
        compiler_params=_params(("arbitrary", "arbitrary", "arbitrary")),
    )(*args)


def _conv_kernel(v_ref, gt_ref, vp_ref, gp_ref, vn_ref, gn_ref, w_ref, b_ref, lg_ref, lb_ref,
                 o_ref, ubuf, sh_ref, *, t, rc):
    i = pl.program_id(1)
    first = i == 0
    last = i == pl.num_programs(1) - 1
    halo = CONV_HALO
    ubuf[halo:halo + t, :] = v_ref[0] * jax.nn.sigmoid(gt_ref[0])
    up = vp_ref[0] * jax.nn.sigmoid(gp_ref[0])
    un = vn_ref[0] * jax.nn.sigmoid(gn_ref[0])
    ubuf[0:halo, :] = jnp.where(first, 0.0, up)
    ubuf[halo + t:2 * halo + t, :] = jnp.where(last, 0.0, un)
    base = halo - A_CONV // 2
    span = t + 8 * ((base + A_CONV - 1) // 8)
    for s in range(1, 8):
        sh_ref[s - 1] = ubuf[s:s + span, :]
    for r0 in range(0, t, rc):
        for g in range(N_GROUPS):
            cs = slice(g * GROUP, (g + 1) * GROUP)
            acc = jnp.zeros((rc, GROUP), F32)
            for k in range(A_CONV):
                s, a8 = (base + k) % 8, 8 * ((base + k) // 8)
                src = ubuf if s == 0 else sh_ref.at[s - 1]
                acc = acc + w_ref[k:k + 1, cs] * src[r0 + a8:r0 + a8 + rc, cs]
            y = acc + b_ref[:, cs]
            mu = jnp.mean(y, axis=-1, keepdims=True)
            dlt = y - mu
            yn = dlt * lax.rsqrt(jnp.mean(dlt * dlt, axis=-1, keepdims=True) + EPS)
            z = yn * lg_ref[:, cs] + lb_ref[:, cs]
            o_ref[0, r0:r0 + rc, cs] = (z * jax.nn.sigmoid(z)).astype(o_ref.dtype)


def _conformer_conv(p, cv_w, cv_b, ln_g, ln_b):
    b, l, _ = p.shape
    c = cv_w.shape[1]
    t = _pick(l, (256, 128))
    rc = 64
    hb = t // CONV_HALO
    nhb = l // CONV_HALO
    main = lambda col: pl.BlockSpec((1, t, c), lambda bi, i: (bi, i, col))
    prev = lambda col: pl.BlockSpec((1, CONV_HALO, c), lambda bi, i: (bi, jnp.maximum(i * hb - 1, 0), col))
    nxt = lambda col: pl.BlockSpec((1, CONV_HALO, c), lambda bi, i: (bi, jnp.minimum((i + 1) * hb, nhb - 1), col))
    vec = pl.BlockSpec((1, c), lambda bi, i: (0, 0))
    return pl.pallas_call(
        functools.partial(_conv_kernel, t=t, rc=rc),
        grid=(b, l // t),
        in_specs=[main(0), main(1), prev(0), prev(1), nxt(0), nxt(1),
                  pl.BlockSpec((A_CONV, c), lambda bi, i: (0, 0)), vec, vec, vec],
        out_specs=pl.BlockSpec((1, t, c), lambda bi, i: (bi, i, 0)),
        out_shape=jax.ShapeDtypeStruct((b, l, c), BF16),
        scratch_shapes=[pltpu.VMEM((t + 2 * CONV_HALO, c), F32),
                        pltpu.VMEM((7, t + 8 * ((CONV_HALO + A_CONV // 2) // 8), c), F32)],
        name="conformer_conv",
        compiler_params=_params(("arbitrary", "arbitrary")),
    )(p, p, p, p, p, p, cv_w, cv_b.reshape(1, c), ln_g.reshape(1, c), ln_b.reshape(1, c))


def _lru_kernel(x_ref, xh_ref, cw_ref, cb_ref, wg_ref, ba_ref, bi_ref, lam_ref, h0_ref,
                h_ref, hl_ref, xbuf, a_buf, b_buf, carry, *, t, reverse):
    i = pl.program_id(1)
    c = x_ref.shape[-1]
    halo = LRU_HALO

    @pl.when(i == 0)
    def _():
        carry[...] = h0_ref[0]

    edge = jnp.where(i == 0, 0.0, xh_ref[0])
    if reverse:
        xbuf[0:t, :] = x_ref[0]
        xbuf[t:t + halo, :] = edge
        base = 0
    else:
        xbuf[0:halo, :] = edge
        xbuf[halo:halo + t, :] = x_ref[0]
        base = halo - (RG_CONV - 1)
    y = jnp.zeros((t, c), F32) + cb_ref[...]
    for k in range(RG_CONV):
        y = y + cw_ref[k:k + 1, :] * xbuf[base + k:base + k + t, :]

    log_sig = jax.nn.log_sigmoid(lam_ref[...])
    for h in range(N_GROUPS):
        cs = slice(h * GROUP, (h + 1) * GROUP)
        yh = y[:, cs]
        gates = jnp.dot(yh.astype(BF16), wg_ref[h], preferred_element_type=F32)
        r = jax.nn.sigmoid(gates[:, :GROUP] + ba_ref[:, cs])
        ig = jax.nn.sigmoid(gates[:, GROUP:] + bi_ref[:, cs])
        log_a = RG_C * r * log_sig[:, cs]
        a = jnp.exp(log_a)
        a_buf[:, cs] = a
        b_buf[:, cs] = jnp.sqrt(1.0 - a * a) * (ig * yh)

    ngroups = t // 8
    row = lax.broadcasted_iota(jnp.int32, (8, c), 0)

    def body(gidx, cr):
        gi = (ngroups - 1 - gidx) if reverse else gidx
        r0 = pl.multiple_of(gi * 8, 8)
        a = a_buf[pl.ds(r0, 8), :]
        bb = b_buf[pl.ds(r0, 8), :]
        for k in (1, 2, 4):
            if reverse:
                a_s = pltpu.roll(a, 8 - k, 0)
                b_s = pltpu.roll(bb, 8 - k, 0)
                valid = row < 8 - k
            else:
                a_s = pltpu.roll(a, k, 0)
                b_s = pltpu.roll(bb, k, 0)
                valid = row >= k
            bb = jnp.where(valid, a * b_s + bb, bb)
            a = jnp.where(valid, a * a_s, a)
        hh = bb + a * cr
        h_ref[0, pl.ds(r0, 8), :] = hh
        edge_row = hh[0:1, :] if reverse else hh[7:8, :]
        return jnp.broadcast_to(edge_row, (8, c))

    cr = lax.fori_loop(0, ngroups, body, carry[...])
    carry[...] = cr
    hl_ref[0] = cr


def _rglru(p, col, h0, conv_w, conv_b, wa, ba, wi, bi, lam, reverse):
    b, l, _ = p.shape
    c = conv_w.shape[1]
    t = _pick(l, (256, 128))
    nt = l // t
    hb = t // LRU_HALO
    nhb = l // LRU_HALO
    if reverse:
        tile = lambda bi_, i: (bi_, nt - 1 - i, col)
        halo = lambda bi_, i: (bi_, jnp.minimum((nt - i) * hb, nhb - 1), col)
        otile = lambda bi_, i: (bi_, nt - 1 - i, 0)
    else:
        tile = lambda bi_, i: (bi_, i, col)
        halo = lambda bi_, i: (bi_, jnp.maximum(i * hb - 1, 0), col)
        otile = lambda bi_, i: (bi_, i, 0)
    wg = jnp.concatenate([wa, wi], axis=-1).astype(BF16)
    vec = pl.BlockSpec((1, c), lambda bi_, i: (0, 0))
    state = pl.BlockSpec((1, 8, c), lambda bi_, i: (bi_, 0, 0))
    return pl.pallas_call(
        functools.partial(_lru_kernel, t=t, reverse=reverse),
        grid=(b, nt),
        in_specs=[pl.BlockSpec((1, t, c), tile),
                  pl.BlockSpec((1, LRU_HALO, c), halo),
                  pl.BlockSpec((RG_CONV, c), lambda bi_, i: (0, 0)), vec,
                  pl.BlockSpec((N_GROUPS, GROUP, 2 * GROUP), lambda bi_, i: (0, 0, 0)),
                  vec, vec, vec, state],
        out_specs=[pl.BlockSpec((1, t, c), otile), state],
        out_shape=[jax.ShapeDtypeStruct((b, l, c), F32), jax.ShapeDtypeStruct((b, 8, c), F32)],
        scratch_shapes=[pltpu.VMEM((t + LRU_HALO, c), F32), pltpu.VMEM((t, c), F32),
                        pltpu.VMEM((t, c), F32), pltpu.VMEM((8, c), F32)],
        name="rglru_rev" if reverse else "rglru_fwd",
        compiler_params=_params(("arbitrary", "arbitrary")),
    )(p, p, conv_w, conv_b.reshape(1, c), wg, ba.reshape(1, c), bi.reshape(1, c), lam.reshape(1, c), h0)


def _mixout_ab_kernel(cv_ref, hf_ref, hr_ref, pg_ref, x_ref, m_ref, g_ref, wa_ref, wb_ref, o_ref):
    rec = (hf_ref[0] + hr_ref[0]) * jax.nn.gelu(pg_ref[0])
    y = jnp.dot(cv_ref[0], wa_ref[...], preferred_element_type=F32)
    y = y + jnp.dot(rec.astype(BF16), wb_ref[...], preferred_element_type=F32)
    o_ref[0] = x_ref[0] + m_ref[0, 2:3, :] * _rms(y, g_ref[...])


def _mixout_ab(conv_out, h_f, h_r, p, x, mod, g, w_out):
    b, l, d = x.shape
    c = conv_out.shape[-1]
    tm = _pick(l, (512, 256, 128))
    half = lambda: pl.BlockSpec((1, tm, c), lambda bi, i: (bi, i, 0))
    wspec = lambda k: pl.BlockSpec((c, d), lambda bi, i: (k, 0))
    return pl.pallas_call(
        _mixout_ab_kernel,
        grid=(b, l // tm),
        in_specs=[half(), half(), half(),
                  pl.BlockSpec((1, tm, c), lambda bi, i: (bi, i, 3)),
                  pl.BlockSpec((1, tm, d), lambda bi, i: (bi, i, 0)),
                  pl.BlockSpec((1, 6, d), lambda bi, i: (bi, 0, 0)),
                  pl.BlockSpec((1, d), lambda bi, i: (0, 0)),
                  wspec(0), wspec(1)],
        out_specs=pl.BlockSpec((1, tm, d), lambda bi, i: (bi, i, 0)),
        out_shape=jax.ShapeDtypeStruct((b, l, d), F32),
        name="mixout_ab",
        compiler_params=_params(("arbitrary", "arbitrary")),
    )(conv_out, h_f, h_r, p, x, mod, g.reshape(1, d), w_out, w_out)


def _mixout_cd_kernel(f_ref, o_att_ref, x_ref, m_ref, g_ref, wa_ref, wb_ref, o_ref):
    y = jnp.dot(f_ref[0].astype(BF16), wa_ref[...], preferred_element_type=F32)
    y = y + jnp.dot(o_att_ref[0], wb_ref[...], preferred_element_type=F32)
    o_ref[0] = x_ref[0] + m_ref[0, 2:3, :] * _rms(y, g_ref[...])


def _mixout_cd(four, att, x, mod, g, w_out):
    b, l, d = x.shape
    c = four.shape[-1]
    tm = _pick(l, (512, 256, 128))
    half = lambda: pl.BlockSpec((1, tm, c), lambda bi, i: (bi, i, 0))
    wspec = lambda k: pl.BlockSpec((c, d), lambda bi, i: (k, 0))
    return pl.pallas_call(
        _mixout_cd_kernel,
        grid=(b, l // tm),
        in_specs=[half(), half(),
                  pl.BlockSpec((1, tm, d), lambda bi, i: (bi, i, 0)),
                  pl.BlockSpec((1, 6, d), lambda bi, i: (bi, 0, 0)),
                  pl.BlockSpec((1, d), lambda bi, i: (0, 0)),
                  wspec(0), wspec(1)],
        out_specs=pl.BlockSpec((1, tm, d), lambda bi, i: (bi, i, 0)),
        out_shape=jax.ShapeDtypeStruct((b, l, d), F32),
        name="mixout_cd",
        compiler_params=_params(("arbitrary", "arbitrary")),
    )(four, att, x, mod, g.reshape(1, d), w_out, w_out)


def _mlp_kernel(x_ref, m_ref, gpre_ref, gpost_ref, w1_ref, w2_ref, o_ref, h_ref, acc_ref, r_ref):
    j = pl.program_id(2)

    @pl.when(j == 0)
    def _():
        _norm_modulate_rows(x_ref, gpre_ref, m_ref, 3, 4, h_ref, r_ref)
        acc_ref[...] = jnp.zeros_like(acc_ref)

    u = jnp.dot(h_ref[...], w1_ref[...], preferred_element_type=F32)
    u = jnp.square(jnp.maximum(u, 0.0))
    acc_ref[...] += jnp.dot(u.astype(BF16), w2_ref[...], preferred_element_type=F32)

    @pl.when(j == pl.num_programs(2) - 1)
    def _():
        _row_rsqrt_ms(lambda rows: acc_ref[rows, :], r_ref)
        gain = m_ref[0, 5:6, :] * gpost_ref[...]
        reps = acc_ref.shape[1] // r_ref.shape[1]

        def body(r0):
            rows = pl.ds(r0, ROW_CHUNK)
            r = pltpu.repeat(r_ref[rows, :], reps, axis=1)
            o_ref[0, rows, :] = x_ref[0, rows, :] + acc_ref[rows, :] * r * gain

        _for_row_chunks(acc_ref.shape[0], body, 4)


def _mlp(x, mod, g_pre, g_post, w1, w2):
    b, l, d = x.shape
    f = w1.shape[1]
    tm = _pick(l, (512, 256, 128))
    tf = _pick(f, (512, 256, 128))
    return pl.pallas_call(
        _mlp_kernel,
        grid=(b, l // tm, f // tf),
        in_specs=[pl.BlockSpec((1, tm, d), lambda bi, i, j: (bi, i, 0)),
                  pl.BlockSpec((1, 6, d), lambda bi, i, j: (bi, 0, 0)),
                  pl.BlockSpec((1, d), lambda bi, i, j: (0, 0)),
                  pl.BlockSpec((1, d), lambda bi, i, j: (0, 0)),
                  pl.BlockSpec((d, tf), lambda bi, i, j: (0, j)),
                  pl.BlockSpec((tf, d), lambda bi, i, j: (j, 0))],
        out_specs=pl.BlockSpec((1, tm, d), lambda bi, i, j: (bi, i, 0)),
        out_shape=jax.ShapeDtypeStruct((b, l, d), F32),
        scratch_shapes=[pltpu.VMEM((tm, d), BF16), pltpu.VMEM((tm, d), F32), pltpu.VMEM((tm, 128), F32)],
        name="mlp",
        compiler_params=_params(("arbitrary", "arbitrary", "arbitrary")),
    )(x, mod, g_pre.reshape(1, d), g_post.reshape(1, d), w1, w2)


def _fourier_tables(l):
    n1 = 1 << (int(math.log2(l)) // 2)
    while l % n1:
        n1 //= 2
    n2 = l // n1
    def cs(n):
        idx = np.arange(n)
        ang = 2.0 * np.pi * ((idx[:, None] * idx[None, :]) % n) / n
        return np.cos(ang), np.sin(ang)
    c1, s1 = cs(n1)
    c2, s2 = cs(n2)
    cc, sc = cs(GROUP)
    delta = 2.0 * np.pi * np.arange(n1)[:, None] / l * np.ones((1, GROUP))
    norm = 1.0 / math.sqrt(l * GROUP)
    chan = np.concatenate([cc, -sc], axis=1)
    f1 = np.concatenate([c1, s1], axis=0)
    f2 = np.concatenate([c2, s2], axis=1) * norm
    return (n1, n2, jnp.asarray(chan, BF16), jnp.asarray(f1, BF16), jnp.asarray(f2, BF16),
            jnp.asarray(np.cos(delta), F32), jnp.asarray(np.sin(delta), F32))


def _fourier_kernel(x_ref, chan_ref, f1_ref, f2_ref, dc_ref, ds_ref, o_ref, are_buf, aim_buf, tw_c, tw_s,
                    *, n1, n2):
    tw_c[...] = jnp.ones_like(tw_c)
    tw_s[...] = jnp.zeros_like(tw_s)

    def stage1(j, _):
        xs = x_ref[0, pl.ds(j, n1, stride=n2), :]
        pq = jnp.dot(xs.astype(BF16), chan_ref[...], preferred_element_type=F32)
        r = jnp.dot(f1_ref[...], pq.astype(BF16), preferred_element_type=F32)
        a_re = r[:n1, :GROUP] + r[n1:, GROUP:]
        a_im = r[:n1, GROUP:] - r[n1:, :GROUP]
        tc = tw_c[...]
        ts = tw_s[...]
        row0 = pl.multiple_of(j * n1, 8)
        are_buf[pl.ds(row0, n1), :] = a_re * tc + a_im * ts
        aim_buf[pl.ds(row0, n1), :] = a_im * tc - a_re * ts
        tw_c[...] = tc * dc_ref[...] - ts * ds_ref[...]
        tw_s[...] = ts * dc_ref[...] + tc * ds_ref[...]
        return 0

    lax.fori_loop(0, n2, stage1, 0, unroll=8)

    def stage2(k1, _):
        re = are_buf[pl.ds(k1, n2, stride=n1), :]
        im = aim_buf[pl.ds(k1, n2, stride=n1), :]
        rhs = jnp.concatenate([re, im], axis=0).astype(BF16)
        o_ref[0, pl.ds(k1, n2, stride=n1), :] = jnp.dot(f2_ref[...], rhs, preferred_element_type=F32)
        return 0

    lax.fori_loop(0, n1, stage2, 0, unroll=8)


def _fourier_mix(p, l):
    b = p.shape[0]
    n1, n2, chan, f1, f2, dc, ds = _fourier_tables(l)
    const = lambda a: pl.BlockSpec(a.shape, lambda bi, g: (0, 0))
    return pl.pallas_call(
        functools.partial(_fourier_kernel, n1=n1, n2=n2),
        grid=(b, N_GROUPS),
        in_specs=[pl.BlockSpec((1, l, GROUP), lambda bi, g: (bi, 0, g)),
                  const(chan), const(f1), const(f2), const(dc), const(ds)],
        out_specs=pl.BlockSpec((1, l, GROUP), lambda bi, g: (bi, 0, g)),
        out_shape=jax.ShapeDtypeStruct((b, l, N_GROUPS * GROUP), F32),
        scratch_shapes=[pltpu.VMEM((l, GROUP), F32), pltpu.VMEM((l, GROUP), F32),
                        pltpu.VMEM((n1, GROUP), F32), pltpu.VMEM((n1, GROUP), F32)],
        name="fourier_mix",
        compiler_params=_params(("arbitrary", "arbitrary"), 62 * 1024 * 1024),
    )(p, chan, f1, f2, dc, ds)


def _rope_tables(l, n_ctx):
    t = np.arange(l)
    inv = ROPE_BASE ** (-np.arange(QK_ROPE // 4, dtype=np.float64) / (QK_ROPE // 4))
    ang = np.concatenate([(t // GRID_W)[:, None] * inv, (t % GRID_W)[:, None] * inv], axis=-1)
    ang = np.concatenate([ang, np.zeros((n_ctx, QK_ROPE // 2))], axis=0)
    cos, sin = np.cos(ang), np.sin(ang)
    return jnp.asarray(np.concatenate([cos, cos, -sin, sin], axis=-1), F32)


def _rope_apply(r, cs):
    tt = r * cs
    ro = tt + pltpu.roll(tt, QK_ROPE, 1)
    lane = lax.broadcasted_iota(jnp.int32, ro.shape, 1)
    return jnp.where(lane < QK_ROPE, ro, 0.0)


def _q_kernel(cq_ref, g_ref, w_ref, cs_ref, o_ref, n_ref):
    @pl.when(pl.program_id(2) == 0)
    def _():
        n_ref[...] = _rms(cq_ref[0], g_ref[...]).astype(BF16)

    res = jnp.dot(n_ref[...], w_ref[0], preferred_element_type=F32)
    q = jnp.concatenate([res[:, :QK_NOPE], _rope_apply(res[:, QK_NOPE:], cs_ref[...])], axis=1)
    o_ref[0, 0] = (q * (math.log2(math.e) * QK_DIM ** -0.5)).astype(o_ref.dtype)


def _mla_queries(p, l, col, q_g, w_uq, cs):
    b = p.shape[0]
    rank = w_uq.shape[0]
    w = w_uq.reshape(rank, N_GROUPS, QK_DIM)
    x1 = w[:, :, QK_NOPE:QK_NOPE + QK_ROPE // 2]
    x2 = w[:, :, QK_NOPE + QK_ROPE // 2:]
    w = jnp.concatenate([w[:, :, :QK_NOPE], x1, x2, x2, x1], axis=-1)
    w = jnp.transpose(w, (1, 0, 2)).astype(BF16)
    tm = _pick(l, (1024, 512, 256, 128))
    return pl.pallas_call(
        _q_kernel,
        grid=(b, l // tm, N_GROUPS),
        in_specs=[pl.BlockSpec((1, tm, rank), lambda bi, i, h: (bi, i, col)),
                  pl.BlockSpec((1, rank), lambda bi, i, h: (0, 0)),
                  pl.BlockSpec((1, rank, QK_PAD), lambda bi, i, h: (h, 0, 0)),
                  pl.BlockSpec((tm, 128), lambda bi, i, h: (i, 0))],
        out_specs=pl.BlockSpec((1, 1, tm, QK_PAD), lambda bi, i, h: (bi, h, i, 0)),
        out_shape=jax.ShapeDtypeStruct((b, N_GROUPS, l, QK_PAD), BF16),
        scratch_shapes=[pltpu.VMEM((tm, rank), BF16)],
        name="mla_q",
        compiler_params=_params(("arbitrary", "arbitrary", "arbitrary")),
    )(p, q_g.reshape(1, rank), w, cs)


def _kv_kernel(ckv_ref, kr_ref, g_ref, w_ref, cs_ref, k_ref, v_ref, n_ref, rope_ref):
    @pl.when(pl.program_id(2) == 0)
    def _():
        n_ref[...] = _rms(ckv_ref[0], g_ref[...]).astype(BF16)
        rope_ref[...] = _rope_apply(kr_ref[0], cs_ref[...]).astype(BF16)

    res = jnp.dot(n_ref[...], w_ref[...], preferred_element_type=F32)
    k_ref[0, 0] = jnp.concatenate([res[:, :QK_NOPE].astype(BF16), rope_ref[...]], axis=1)
    v_ref[0, 0] = res[:, QK_NOPE:].astype(BF16)


def _mla_keys_values(p, col_kv, col_kr, kv_g, w_ukv, cs):
    b, lk, _ = p.shape
    rank = w_ukv.shape[0]
    tm = _pick(lk, (1280, 640, 256, 128))
    hw = QK_NOPE + V_DIM
    return pl.pallas_call(
        _kv_kernel,
        grid=(b, lk // tm, N_GROUPS),
        in_specs=[pl.BlockSpec((1, tm, rank), lambda bi, i, h: (bi, i, col_kv)),
                  pl.BlockSpec((1, tm, 128), lambda bi, i, h: (bi, i, col_kr)),
                  pl.BlockSpec((1, rank), lambda bi, i, h: (0, 0)),
                  pl.BlockSpec((rank, hw), lambda bi, i, h: (0, h)),
                  pl.BlockSpec((tm, 128), lambda bi, i, h: (i, 0))],
        out_specs=[pl.BlockSpec((1, 1, tm, QK_PAD), lambda bi, i, h: (bi, h, i, 0)),
                   pl.BlockSpec((1, 1, tm, V_DIM), lambda bi, i, h: (bi, h, i, 0))],
        out_shape=[jax.ShapeDtypeStruct((b, N_GROUPS, lk, QK_PAD), BF16),
                   jax.ShapeDtypeStruct((b, N_GROUPS, lk, V_DIM), BF16)],
        scratch_shapes=[pltpu.VMEM((tm, rank), BF16), pltpu.VMEM((tm, 128), BF16)],
        name="mla_kv",
        compiler_params=_params(("arbitrary", "arbitrary", "arbitrary")),
    )(p, p, kv_g.reshape(1, rank), w_ukv.astype(BF16), cs)


def _attn_kernel(q_ref, k_ref, v_ref, o_ref, m_ref, l_ref, acc_ref, sa_ref, sb_ref, *, tks, n_sub, rem):
    q = q_ref[0, 0]

    def scores(r0, width):
        return lax.dot_general(q, k_ref[0, 0, pl.ds(r0, width), :], (((1,), (1,)), ((), ())),
                               preferred_element_type=F32)

    def update(s, r0, width):
        m_prev = m_ref[...]
        m_next = jnp.maximum(m_prev, jnp.max(s, axis=1, keepdims=True))
        p = jnp.exp2(s - pltpu.repeat(m_next, width // 128, axis=1))
        alpha = jnp.exp2(m_prev - m_next)
        psum = p[:, :128]
        for blk in range(1, width // 128):
            psum = psum + p[:, blk * 128:(blk + 1) * 128]
        l_ref[...] = alpha * l_ref[...] + psum
        pv = jnp.dot(p.astype(BF16), v_ref[0, 0, pl.ds(r0, width), :], preferred_element_type=F32)
        acc_ref[...] = alpha * acc_ref[...] + pv
        m_ref[...] = m_next

    def start(j):
        return pl.multiple_of(j * tks, tks)

    m_ref[...] = jnp.full_like(m_ref, -jnp.inf)
    l_ref[...] = jnp.zeros_like(l_ref)
    acc_ref[...] = jnp.zeros_like(acc_ref)
    sa_ref[...] = scores(0, tks)

    def pair(i, _):
        sb_ref[...] = scores(start(2 * i + 1), tks)
        update(sa_ref[...], start(2 * i), tks)
        sa_ref[...] = scores(start(2 * i + 2), tks)
        update(sb_ref[...], start(2 * i + 1), tks)
        return 0

    n_pairs = n_sub // 2
    lax.fori_loop(0, n_pairs - 1, pair, 0, unroll=5 if (n_pairs - 1) % 5 == 0 else 1)
    done = 2 * (n_pairs - 1)
    sb_ref[...] = scores(done * tks + tks, tks)
    update(sa_ref[...], done * tks, tks)
    tail = [(j * tks, tks) for j in range(done + 2, n_sub)] + ([(n_sub * tks, rem)] if rem else [])
    bufs = [sa_ref, sb_ref]
    pending = (sb_ref, (done + 1) * tks, tks)
    for idx, (r0, width) in enumerate(tail):
        nxt = bufs[idx % 2]
        nxt[:, :width] = scores(r0, width)
        update(pending[0][:, :pending[2]], pending[1], pending[2])
        pending = (nxt, r0, width)
    update(pending[0][:, :pending[2]], pending[1], pending[2])
    l = jnp.sum(l_ref[...], axis=1, keepdims=True)
    o_ref[0] = (acc_ref[...] / l).astype(o_ref.dtype)


def _attention(q, k, v):
    b, h, l, _ = q.shape
    lk = k.shape[2]
    tq = _pick(l, (1024, 512, 256, 128))
    tks = 512
    n_sub, rem = lk // tks, lk % tks
    assert n_sub >= 2 and rem % 128 == 0
    return pl.pallas_call(
        functools.partial(_attn_kernel, tks=tks, n_sub=n_sub, rem=rem),
        grid=(b, h, l // tq),
        in_specs=[pl.BlockSpec((1, 1, tq, QK_PAD), lambda bi, hi, i: (bi, hi, i, 0)),
                  pl.BlockSpec((1, 1, lk, QK_PAD), lambda bi, hi, i: (bi, hi, 0, 0)),
                  pl.BlockSpec((1, 1, lk, V_DIM), lambda bi, hi, i: (bi, hi, 0, 0))],
        out_specs=pl.BlockSpec((1, tq, V_DIM), lambda bi, hi, i: (bi, i, hi)),
        out_shape=jax.ShapeDtypeStruct((b, l, h * V_DIM), BF16),
        scratch_shapes=[pltpu.VMEM((tq, 128), F32), pltpu.VMEM((tq, 128), F32), pltpu.VMEM((tq, V_DIM), F32),
                        pltpu.VMEM((tq, tks), F32), pltpu.VMEM((tq, tks), F32)],
        name="flash_attention",
        compiler_params=_params(("arbitrary", "arbitrary", "arbitrary")),
    )(q, k, v)


def _conv_lru_layer(xc, xl, ml, mc, ng, w1, w2, w_in, w_out, cv_w, cv_b, cv_g, cv_beta,
                    rg_conv_w, rg_conv_b, rg_wa, rg_ba, rg_wi, rg_bi, rg_lambda, need_ctx):
    w_in = w_in.astype(BF16)
    w_out = w_out.astype(BF16)
    w1 = w1.astype(BF16)
    w2 = w2.astype(BF16)
    pc = _norm_matmul(xc, mc, ng[0], w_in, 0, 1)
    plat = _norm_matmul(xl, ml, ng[0], w_in, 0, 1)
    c = cv_w.shape[1]
    zero = jnp.zeros((xl.shape[0], 8, c), F32)
    hs_c, hs_l = [], []
    for d, reverse in enumerate((False, True)):
        args = (rg_conv_w[d], rg_conv_b[d], rg_wa[d], rg_ba[d], rg_wi[d], rg_bi[d], rg_lambda[d], reverse)
        h_c, last_c = _rglru(pc, 2, zero, *args)
        h_l, _ = _rglru(plat, 2, last_c, *args)
        hs_c.append(h_c)
        hs_l.append(h_l)

    def finish(p, hs, x, mod):
        conv_out = _conformer_conv(p, cv_w, cv_b, cv_g, cv_beta)
        x = _mixout_ab(conv_out, hs[0], hs[1], p, x, mod, ng[1], w_out)
        return _mlp(x, mod, ng[2], ng[3], w1, w2)

    xl = finish(plat, hs_l, xl, ml)
    if need_ctx:
        xc = finish(pc, hs_c, xc, mc)
    return xc, xl


def _fourier_mla_layer(xc, xl, ml, mc, ng, w1, w2, w_in, w_out, q_g, kv_g, w_uq, w_ukv):
    b, l, d = xl.shape
    n_ctx = xc.shape[1]
    c = N_GROUPS * GROUP
    q_rank = q_g.shape[0]
    kv_rank = kv_g.shape[0]
    kr = w_in[:, c + q_rank + kv_rank:]
    kr1, kr2 = kr[:, :QK_ROPE // 2], kr[:, QK_ROPE // 2:]
    n_used = c + q_rank + kv_rank + 2 * QK_ROPE
    n_pad = -n_used % 256
    w_in_p = jnp.concatenate([w_in[:, :c + q_rank + kv_rank], kr1, kr2, kr2, kr1,
                              jnp.zeros((d, n_pad), w_in.dtype)], axis=1).astype(BF16)
    lk = l + n_ctx
    p = _norm_matmul(xl, ml, ng[0], w_in_p, 0, 1, out_rows=lk)
    tmc = _pick(n_ctx, (1024, 512, 256, 128))
    p = _norm_matmul(xc, mc, ng[0], w_in_p, 0, 1, out_rows=lk, row_block_off=l // tmc, into=p)
    cs = _rope_tables(l, n_ctx)
    four = _fourier_mix(p, l)
    q = _mla_queries(p, l, c // q_rank, q_g, w_uq, cs)
    k, v = _mla_keys_values(p, (c + q_rank) // kv_rank, (c + q_rank + kv_rank) // 128, kv_g, w_ukv, cs)
    att = _attention(q, k, v)
    xl = _mixout_cd(four, att, xl, ml, ng[1], w_out.astype(BF16))
    return _mlp(xl, ml, ng[2], ng[3], w1.astype(BF16), w2.astype(BF16))


def kernel(x, c, ctx, c_ctx, mod_w, mod_b, norm_g, mlp_w1, mlp_w2, ab_w_in, ab_w_out, cv_w, cv_b, cv_norm_g, cv_norm_b, rg_conv_w, rg_conv_b, rg_wa, rg_ba, rg_wi, rg_bi, rg_lambda, cd_w_in, cd_w_out, mla_q_norm_g, mla_kv_norm_g, mla_w_uq, mla_w_ukv):
    b, l, d = x.shape
    depth = mod_w.shape[0]
    assert b + 1 <= 8 and l % GRID_W == 0
    cond8 = jnp.concatenate([c, c_ctx[None, :], jnp.zeros((8 - b - 1, d), F32)], axis=0)
    mods = _modulation(cond8, mod_w, mod_b)
    xc, xl = ctx, x
    for i in range(depth):
        need_ctx = i < depth - 1
        ml = mods[i, :b].reshape(b, 6, d)
        mc = jnp.broadcast_to(mods[i, b].reshape(1, 6, d), (b, 6, d))
        j = i // 2
        if i % 2 == 0:
            xc, xl = _conv_lru_layer(xc, xl, ml, mc, norm_g[i], mlp_w1[i], mlp_w2[i], ab_w_in[j], ab_w_out[j],
                                     cv_w[j], cv_b[j], cv_norm_g[j], cv_norm_b[j], rg_conv_w[j], rg_conv_b[j],
                                     rg_wa[j], rg_ba[j], rg_wi[j], rg_bi[j], rg_lambda[j], need_ctx)
        else:
            xl = _fourier_mla_layer(xc, xl, ml, mc, norm_g[i], mlp_w1[i], mlp_w2[i], cd_w_in[j], cd_w_out[j],
                                    mla_q_norm_g[j], mla_kv_norm_g[j], mla_w_uq[j], mla_w_ukv[j])
    return xl
```

```python
import functools
import math

import numpy as np
import jax
import jax.numpy as jnp
from jax import lax
from jax.experimental import pallas as pl
from jax.experimental.pallas import tpu as pltpu

F32 = jnp.float32
BF16 = jnp.bfloat16

EPS = 1e-6
GRID_W = 64
N_GROUPS = 8
GROUP = 128
A_CONV = 31
RG_CONV = 4
RG_C = 8.0
QK_NOPE = 128
QK_ROPE = 64
QK_DIM = QK_NOPE + QK_ROPE
V_DIM = 128
ROPE_BASE = 10000.0
QK_PAD = 256
CONV_HALO = 16
LRU_HALO = 8
VMEM_LIMIT = 56 * 1024 * 1024


def _params(sem, vmem=VMEM_LIMIT):
    return pltpu.CompilerParams(dimension_semantics=sem, vmem_limit_bytes=vmem)


def _pick(n, candidates):
    for c in candidates:
        if n % c == 0:
            return c
    raise ValueError(f"no tile for {n} in {candidates}")


def _rms(x, g):
    return x * lax.rsqrt(jnp.mean(x * x, axis=-1, keepdims=True) + EPS) * g


ROW_CHUNK = 16


def _for_row_chunks(n_rows, body, unroll):
    def step(i, _):
        body(pl.multiple_of(i * ROW_CHUNK, ROW_CHUNK))
        return 0
    lax.fori_loop(0, n_rows // ROW_CHUNK, step, 0, unroll=unroll)


def _row_rsqrt_ms(src, r_ref):
    def body(r0):
        rows = pl.ds(r0, ROW_CHUNK)
        x = src(rows)
        r = lax.rsqrt(jnp.mean(x * x, axis=-1, keepdims=True) + EPS)
        r_ref[rows, :] = jnp.broadcast_to(r, (ROW_CHUNK, r_ref.shape[1]))

    _for_row_chunks(r_ref.shape[0], body, 8)


def _norm_modulate_rows(x_ref, g_ref, m_ref, shift_row, scale_row, h_ref, r_ref):
    _row_rsqrt_ms(lambda rows: x_ref[0, rows, :], r_ref)
    gain = g_ref[...] * (1.0 + m_ref[0, scale_row:scale_row + 1, :])
    shift = m_ref[0, shift_row:shift_row + 1, :]
    reps = h_ref.shape[1] // r_ref.shape[1]

    def body(r0):
        rows = pl.ds(r0, ROW_CHUNK)
        r = pltpu.repeat(r_ref[rows, :], reps, axis=1)
        h_ref[rows, :] = (x_ref[0, rows, :] * r * gain + shift).astype(h_ref.dtype)

    _for_row_chunks(h_ref.shape[0], body, 4)


def _mod_kernel(c_ref, w_ref, b_ref, o_ref):
    c = c_ref[...]
    s = c * jax.nn.sigmoid(c)
    o_ref[0] = jnp.dot(s, w_ref[0], preferred_element_type=F32,
                       precision=lax.Precision.HIGHEST) + b_ref[0]


def _modulation(cond8, mod_w, mod_b):
    depth, d, n = mod_w.shape
    tn = _pick(n, (1024, 512, 256, 128))
    return pl.pallas_call(
        _mod_kernel,
        grid=(depth, n // tn),
        in_specs=[pl.BlockSpec((8, d), lambda l, j: (0, 0)),
                  pl.BlockSpec((1, d, tn), lambda l, j: (l, 0, j)),
                  pl.BlockSpec((1, 1, tn), lambda l, j: (l, 0, j))],
        out_specs=pl.BlockSpec((1, 8, tn), lambda l, j: (l, 0, j)),
        out_shape=jax.ShapeDtypeStruct((depth, 8, n), F32),
        name="modulation",
        compiler_params=_params(("arbitrary", "arbitrary")),
    )(cond8, mod_w, mod_b.reshape(depth, 1, n))


def _norm_matmul_kernel(*refs, shift_row, scale_row, aliased):
    if aliased:
        x_ref, m_ref, g_ref, w_ref, _, o_ref, h_ref, r_ref = refs
    else:
        x_ref, m_ref, g_ref, w_ref, o_ref, h_ref, r_ref = refs

    @pl.when(pl.program_id(2) == 0)
    def _():
        _norm_modulate_rows(x_ref, g_ref, m_ref, shift_row, scale_row, h_ref, r_ref)

    o_ref[0] = jnp.dot(h_ref[...], w_ref[...], preferred_element_type=F32).astype(o_ref.dtype)


def _norm_matmul(x, mod, g, w, shift_row, scale_row, out_rows=None, row_block_off=0, into=None):
    b, l, d = x.shape
    n = w.shape[1]
    tm = _pick(l, (1024, 512, 256, 128))
    tn = _pick(n, (1024, 768, 512, 256, 128))
    out_rows = l if out_rows is None else out_rows
    off = row_block_off
    in_specs = [pl.BlockSpec((1, tm, d), lambda bi, i, j: (bi, i, 0)),
                pl.BlockSpec((1, 6, d), lambda bi, i, j: (bi, 0, 0)),
                pl.BlockSpec((1, d), lambda bi, i, j: (0, 0)),
                pl.BlockSpec((d, tn), lambda bi, i, j: (0, j))]
    args = [x, mod, g.reshape(1, d), w]
    aliases = {}
    if into is not None:
        in_specs.append(pl.BlockSpec(memory_space=pl.ANY))
        args.append(into)
        aliases = {4: 0}
    return pl.pallas_call(
        functools.partial(_norm_matmul_kernel, shift_row=shift_row, scale_row=scale_row,
                          aliased=into is not None),
        grid=(b, l // tm, n // tn),
        in_specs=in_specs,
        out_specs=pl.BlockSpec((1, tm, tn), lambda bi, i, j: (bi, i + off, j)),
        out_shape=jax.ShapeDtypeStruct((b, out_rows, n), F32),
        scratch_shapes=[pltpu.VMEM((tm, d), BF16), pltpu.VMEM((tm, 128), F32)],
        input_output_aliases=aliases,
        name="norm_matmul",
        compiler_params=_params(("arbitrary", "arbitrary", "arbitrary")),
    )(*args)


def _conv_kernel(v_ref, gt_ref, vp_ref, gp_ref, vn_ref, gn_ref, w_ref, b_ref, lg_ref, lb_ref,
                 o_ref, ubuf, sh_ref, *, t, rc):
    i = pl.program_id(1)
    first = i == 0
    last = i == pl.num_programs(1) - 1
    halo = CONV_HALO
    ubuf[halo:halo + t, :] = v_ref[0] * jax.nn.sigmoid(gt_ref[0])
    up = vp_ref[0] * jax.nn.sigmoid(gp_ref[0])
    un = vn_ref[0] * jax.nn.sigmoid(gn_ref[0])
    ubuf[0:halo, :] = jnp.where(first, 0.0, up)
    ubuf[halo + t:2 * halo + t, :] = jnp.where(last, 0.0, un)
    base = halo - A_CONV // 2
    span = t + 8 * ((base + A_CONV - 1) // 8)
    for s in range(1, 8):
        sh_ref[s - 1] = ubuf[s:s + span, :]
    for r0 in range(0, t, rc):
        for g in range(N_GROUPS):
            cs = slice(g * GROUP, (g + 1) * GROUP)
            acc = jnp.zeros((rc, GROUP), F32)
            for k in range(A_CONV):
                s, a8 = (base + k) % 8, 8 * ((base + k) // 8)
                src = ubuf if s == 0 else sh_ref.at[s - 1]
                acc = acc + w_ref[k:k + 1, cs] * src[r0 + a8:r0 + a8 + rc, cs]
            y = acc + b_ref[:, cs]
            mu = jnp.mean(y, axis=-1, keepdims=True)
            dlt = y - mu
            yn = dlt * lax.rsqrt(jnp.mean(dlt * dlt, axis=-1, keepdims=True) + EPS)
            z = yn * lg_ref[:, cs] + lb_ref[:, cs]
            o_ref[0, r0:r0 + rc, cs] = (z * jax.nn.sigmoid(z)).astype(o_ref.dtype)


def _conformer_conv(p, cv_w, cv_b, ln_g, ln_b):
    b, l, _ = p.shape
    c = cv_w.shape[1]
    t = _pick(l, (256, 128))
    rc = 64
    hb = t // CONV_HALO
    nhb = l // CONV_HALO
    main = lambda col: pl.BlockSpec((1, t, c), lambda bi, i: (bi, i, col))
    prev = lambda col: pl.BlockSpec((1, CONV_HALO, c), lambda bi, i: (bi, jnp.maximum(i * hb - 1, 0), col))
    nxt = lambda col: pl.BlockSpec((1, CONV_HALO, c), lambda bi, i: (bi, jnp.minimum((i + 1) * hb, nhb - 1), col))
    vec = pl.BlockSpec((1, c), lambda bi, i: (0, 0))
    return pl.pallas_call(
        functools.partial(_conv_kernel, t=t, rc=rc),
        grid=(b, l // t),
        in_specs=[main(0), main(1), prev(0), prev(1), nxt(0), nxt(1),
                  pl.BlockSpec((A_CONV, c), lambda bi, i: (0, 0)), vec, vec, vec],
        out_specs=pl.BlockSpec((1, t, c), lambda bi, i: (bi, i, 0)),
        out_shape=jax.ShapeDtypeStruct((b, l, c), BF16),
        scratch_shapes=[pltpu.VMEM((t + 2 * CONV_HALO, c), F32),
                        pltpu.VMEM((7, t + 8 * ((CONV_HALO + A_CONV // 2) // 8), c), F32)],
        name="conformer_conv",
        compiler_params=_params(("arbitrary", "arbitrary")),
    )(p, p, p, p, p, p, cv_w, cv_b.reshape(1, c), ln_g.reshape(1, c), ln_b.reshape(1, c))


def _lru_kernel(x_ref, xh_ref, cw_ref, cb_ref, wg_ref, ba_ref, bi_ref, lam_ref, h0_ref,
                h_ref, hl_ref, xbuf, a_buf, b_buf, carry, *, t, reverse):
    i = pl.program_id(1)
    c = x_ref.shape[-1]
    halo = LRU_HALO

    @pl.when(i == 0)
    def _():
        carry[...] = h0_ref[0]

    edge = jnp.where(i == 0, 0.0, xh_ref[0])
    if reverse:
        xbuf[0:t, :] = x_ref[0]
        xbuf[t:t + halo, :] = edge
        base = 0
    else:
        xbuf[0:halo, :] = edge
        xbuf[halo:halo + t, :] = x_ref[0]
        base = halo - (RG_CONV - 1)
    y = jnp.zeros((t, c), F32) + cb_ref[...]
    for k in range(RG_CONV):
        y = y + cw_ref[k:k + 1, :] * xbuf[base + k:base + k + t, :]

    log_sig = jax.nn.log_sigmoid(lam_ref[...])
    for h in range(N_GROUPS):
        cs = slice(h * GROUP, (h + 1) * GROUP)
        yh = y[:, cs]
        gates = jnp.dot(yh.astype(BF16), wg_ref[h], preferred_element_type=F32)
        r = jax.nn.sigmoid(gates[:, :GROUP] + ba_ref[:, cs])
        ig = jax.nn.sigmoid(gates[:, GROUP:] + bi_ref[:, cs])
        log_a = RG_C * r * log_sig[:, cs]
        a = jnp.exp(log_a)
        a_buf[:, cs] = a
        b_buf[:, cs] = jnp.sqrt(1.0 - a * a) * (ig * yh)

    ngroups = t // 8
    row = lax.broadcasted_iota(jnp.int32, (8, c), 0)

    def body(gidx, cr):
        gi = (ngroups - 1 - gidx) if reverse else gidx
        r0 = pl.multiple_of(gi * 8, 8)
        a = a_buf[pl.ds(r0, 8), :]
        bb = b_buf[pl.ds(r0, 8), :]
        for k in (1, 2, 4):
            if reverse:
                a_s = pltpu.roll(a, 8 - k, 0)
                b_s = pltpu.roll(bb, 8 - k, 0)
                valid = row < 8 - k
            else:
                a_s = pltpu.roll(a, k, 0)
                b_s = pltpu.roll(bb, k, 0)
                valid = row >= k
            bb = jnp.where(valid, a * b_s + bb, bb)
            a = jnp.where(valid, a * a_s, a)
        hh = bb + a * cr
        h_ref[0, pl.ds(r0, 8), :] = hh
        edge_row = hh[0:1, :] if reverse else hh[7:8, :]
        return jnp.broadcast_to(edge_row, (8, c))

    cr = lax.fori_loop(0, ngroups, body, carry[...])
    carry[...] = cr
    hl_ref[0] = cr


def _rglru(p, col, h0, conv_w, conv_b, wa, ba, wi, bi, lam, reverse):
    b, l, _ = p.shape
    c = conv_w.shape[1]
    t = _pick(l, (256, 128))
    nt = l // t
    hb = t // LRU_HALO
    nhb = l // LRU_HALO
    if reverse:
        tile = lambda bi_, i: (bi_, nt - 1 - i, col)
        halo = lambda bi_, i: (bi_, jnp.minimum((nt - i) * hb, nhb - 1), col)
        otile = lambda bi_, i: (bi_, nt - 1 - i, 0)
    else:
        tile = lambda bi_, i: (bi_, i, col)
        halo = lambda bi_, i: (bi_, jnp.maximum(i * hb - 1, 0), col)
        otile = lambda bi_, i: (bi_, i, 0)
    wg = jnp.concatenate([wa, wi], axis=-1).astype(BF16)
    vec = pl.BlockSpec((1, c), lambda bi_, i: (0, 0))
    state = pl.BlockSpec((1, 8, c), lambda bi_, i: (bi_, 0, 0))
    return pl.pallas_call(
        functools.partial(_lru_kernel, t=t, reverse=reverse),
        grid=(b, nt),
        in_specs=[pl.BlockSpec((1, t, c), tile),
                  pl.BlockSpec((1, LRU_HALO, c), halo),
                  pl.BlockSpec((RG_CONV, c), lambda bi_, i: (0, 0)), vec,
                  pl.BlockSpec((N_GROUPS, GROUP, 2 * GROUP), lambda bi_, i: (0, 0, 0)),
                  vec, vec, vec, state],
        out_specs=[pl.BlockSpec((1, t, c), otile), state],
        out_shape=[jax.ShapeDtypeStruct((b, l, c), F32), jax.ShapeDtypeStruct((b, 8, c), F32)],
        scratch_shapes=[pltpu.VMEM((t + LRU_HALO, c), F32), pltpu.VMEM((t, c), F32),
                        pltpu.VMEM((t, c), F32), pltpu.VMEM((8, c), F32)],
        name="rglru_rev" if reverse else "rglru_fwd",
        compiler_params=_params(("arbitrary", "arbitrary")),
    )(p, p, conv_w, conv_b.reshape(1, c), wg, ba.reshape(1, c), bi.reshape(1, c), lam.reshape(1, c), h0)


def _mixout_ab_kernel(cv_ref, hf_ref, hr_ref, pg_ref, x_ref, m_ref, g_ref, wa_ref, wb_ref, o_ref):
    rec = (hf_ref[0] + hr_ref[0]) * jax.nn.gelu(pg_ref[0])
    y = jnp.dot(cv_ref[0], wa_ref[...], preferred_element_type=F32)
    y = y + jnp.dot(rec.astype(BF16), wb_ref[...], preferred_element_type=F32)
    o_ref[0] = x_ref[0] + m_ref[0, 2:3, :] * _rms(y, g_ref[...])


def _mixout_ab(conv_out, h_f, h_r, p, x, mod, g, w_out):
    b, l, d = x.shape
    c = conv_out.shape[-1]
    tm = _pick(l, (512, 256, 128))
    half = lambda: pl.BlockSpec((1, tm, c), lambda bi, i: (bi, i, 0))
    wspec = lambda k: pl.BlockSpec((c, d), lambda bi, i: (k, 0))
    return pl.pallas_call(
        _mixout_ab_kernel,
        grid=(b, l // tm),
        in_specs=[half(), half(), half(),
                  pl.BlockSpec((1, tm, c), lambda bi, i: (bi, i, 3)),
                  pl.BlockSpec((1, tm, d), lambda bi, i: (bi, i, 0)),
                  pl.BlockSpec((1, 6, d), lambda bi, i: (bi, 0, 0)),
                  pl.BlockSpec((1, d), lambda bi, i: (0, 0)),
                  wspec(0), wspec(1)],
        out_specs=pl.BlockSpec((1, tm, d), lambda bi, i: (bi, i, 0)),
        out_shape=jax.ShapeDtypeStruct((b, l, d), F32),
        name="mixout_ab",
        compiler_params=_params(("arbitrary", "arbitrary")),
    )(conv_out, h_f, h_r, p, x, mod, g.reshape(1, d), w_out, w_out)


def _mixout_cd_kernel(f_ref, o_att_ref, x_ref, m_ref, g_ref, wa_ref, wb_ref, o_ref):
    y = jnp.dot(f_ref[0].astype(BF16), wa_ref[...], preferred_element_type=F32)
    y = y + jnp.dot(o_att_ref[0], wb_ref[...], preferred_element_type=F32)
    o_ref[0] = x_ref[0] + m_ref[0, 2:3, :] * _rms(y, g_ref[...])


def _mixout_cd(four, att, x, mod, g, w_out):
    b, l, d = x.shape
    c = four.shape[-1]
    tm = _pick(l, (512, 256, 128))
    half = lambda: pl.BlockSpec((1, tm, c), lambda bi, i: (bi, i, 0))
    wspec = lambda k: pl.BlockSpec((c, d), lambda bi, i: (k, 0))
    return pl.pallas_call(
        _mixout_cd_kernel,
        grid=(b, l // tm),
        in_specs=[half(), half(),
                  pl.BlockSpec((1, tm, d), lambda bi, i: (bi, i, 0)),
                  pl.BlockSpec((1, 6, d), lambda bi, i: (bi, 0, 0)),
                  pl.BlockSpec((1, d), lambda bi, i: (0, 0)),
                  wspec(0), wspec(1)],
        out_specs=pl.BlockSpec((1, tm, d), lambda bi, i: (bi, i, 0)),
        out_shape=jax.ShapeDtypeStruct((b, l, d), F32),
        name="mixout_cd",
        compiler_params=_params(("arbitrary", "arbitrary")),
    )(four, att, x, mod, g.reshape(1, d), w_out, w_out)


def _mlp_kernel(x_ref, m_ref, gpre_ref, gpost_ref, w1_ref, w2_ref, o_ref, h_ref, acc_ref, r_ref):
    j = pl.program_id(2)

    @pl.when(j == 0)
    def _():
        _norm_modulate_rows(x_ref, gpre_ref, m_ref, 3, 4, h_ref, r_ref)
        acc_ref[...] = jnp.zeros_like(acc_ref)

    u = jnp.dot(h_ref[...], w1_ref[...], preferred_element_type=F32)
    u = jnp.square(jnp.maximum(u, 0.0))
    acc_ref[...] += jnp.dot(u.astype(BF16), w2_ref[...], preferred_element_type=F32)

    @pl.when(j == pl.num_programs(2) - 1)
    def _():
        _row_rsqrt_ms(lambda rows: acc_ref[rows, :], r_ref)
        gain = m_ref[0, 5:6, :] * gpost_ref[...]
        reps = acc_ref.shape[1] // r_ref.shape[1]

        def body(r0):
            rows = pl.ds(r0, ROW_CHUNK)
            r = pltpu.repeat(r_ref[rows, :], reps, axis=1)
            o_ref[0, rows, :] = x_ref[0, rows, :] + acc_ref[rows, :] * r * gain

        _for_row_chunks(acc_ref.shape[0], body, 4)


def _mlp(x, mod, g_pre, g_post, w1, w2, layer):
    b, l, d = x.shape
    f = w1.shape[2]
    tm = _pick(l, (512, 256, 128))
    tf = _pick(f, (1024, 512, 256, 128))
    return pl.pallas_call(
        _mlp_kernel,
        grid=(b, l // tm, f // tf),
        in_specs=[pl.BlockSpec((1, tm, d), lambda bi, i, j: (bi, i, 0)),
                  pl.BlockSpec((1, 6, d), lambda bi, i, j: (bi, 0, 0)),
                  pl.BlockSpec((1, d), lambda bi, i, j: (0, 0)),
                  pl.BlockSpec((1, d), lambda bi, i, j: (0, 0)),
                  pl.BlockSpec((None, d, tf), lambda bi, i, j: (layer, 0, j)),
                  pl.BlockSpec((None, tf, d), lambda bi, i, j: (layer, j, 0))],
        out_specs=pl.BlockSpec((1, tm, d), lambda bi, i, j: (bi, i, 0)),
        out_shape=jax.ShapeDtypeStruct((b, l, d), F32),
        scratch_shapes=[pltpu.VMEM((tm, d), BF16), pltpu.VMEM((tm, d), F32), pltpu.VMEM((tm, 128), F32)],
        name="mlp",
        compiler_params=_params(("arbitrary", "arbitrary", "arbitrary")),
    )(x, mod, g_pre.reshape(1, d), g_post.reshape(1, d), w1, w2)


def _fourier_tables(l):
    n1 = 1 << (int(math.log2(l)) // 2)
    while l % n1:
        n1 //= 2
    n2 = l // n1
    def cs(n):
        idx = np.arange(n)
        ang = 2.0 * np.pi * ((idx[:, None] * idx[None, :]) % n) / n
        return np.cos(ang), np.sin(ang)
    c1, s1 = cs(n1)
    c2, s2 = cs(n2)
    cc, sc = cs(GROUP)
    delta = 2.0 * np.pi * np.arange(n1)[:, None] / l * np.ones((1, GROUP))
    norm = 1.0 / math.sqrt(l * GROUP)
    chan = np.concatenate([cc, -sc], axis=1)
    f1 = np.concatenate([c1, s1], axis=0)
    f2 = np.concatenate([c2, s2], axis=1) * norm
    return (n1, n2, jnp.asarray(chan, BF16), jnp.asarray(f1, BF16), jnp.asarray(f2, BF16),
            jnp.asarray(np.cos(delta), F32), jnp.asarray(np.sin(delta), F32))


def _fourier_kernel(x_ref, chan_ref, f1_ref, f2_ref, dc_ref, ds_ref, o_ref, are_buf, aim_buf, tw_c, tw_s,
                    *, n1, n2):
    tw_c[...] = jnp.ones_like(tw_c)
    tw_s[...] = jnp.zeros_like(tw_s)

    def stage1(j, _):
        xs = x_ref[0, pl.ds(j, n1, stride=n2), :]
        pq = jnp.dot(xs.astype(BF16), chan_ref[...], preferred_element_type=F32)
        r = jnp.dot(f1_ref[...], pq.astype(BF16), preferred_element_type=F32)
        a_re = r[:n1, :GROUP] + r[n1:, GROUP:]
        a_im = r[:n1, GROUP:] - r[n1:, :GROUP]
        tc = tw_c[...]
        ts = tw_s[...]
        row0 = pl.multiple_of(j * n1, 8)
        are_buf[pl.ds(row0, n1), :] = a_re * tc + a_im * ts
        aim_buf[pl.ds(row0, n1), :] = a_im * tc - a_re * ts
        tw_c[...] = tc * dc_ref[...] - ts * ds_ref[...]
        tw_s[...] = ts * dc_ref[...] + tc * ds_ref[...]
        return 0

    lax.fori_loop(0, n2, stage1, 0, unroll=8)

    def stage2(k1, _):
        re = are_buf[pl.ds(k1, n2, stride=n1), :]
        im = aim_buf[pl.ds(k1, n2, stride=n1), :]
        rhs = jnp.concatenate([re, im], axis=0).astype(BF16)
        o_ref[0, pl.ds(k1, n2, stride=n1), :] = jnp.dot(f2_ref[...], rhs, preferred_element_type=F32)
        return 0

    lax.fori_loop(0, n1, stage2, 0, unroll=8)


def _fourier_mix(p, l):
    b = p.shape[0]
    n1, n2, chan, f1, f2, dc, ds = _fourier_tables(l)
    const = lambda a: pl.BlockSpec(a.shape, lambda bi, g: (0, 0))
    return pl.pallas_call(
        functools.partial(_fourier_kernel, n1=n1, n2=n2),
        grid=(b, N_GROUPS),
        in_specs=[pl.BlockSpec((1, l, GROUP), lambda bi, g: (bi, 0, g)),
                  const(chan), const(f1), const(f2), const(dc), const(ds)],
        out_specs=pl.BlockSpec((1, l, GROUP), lambda bi, g: (bi, 0, g)),
        out_shape=jax.ShapeDtypeStruct((b, l, N_GROUPS * GROUP), F32),
        scratch_shapes=[pltpu.VMEM((l, GROUP), F32), pltpu.VMEM((l, GROUP), F32),
                        pltpu.VMEM((n1, GROUP), F32), pltpu.VMEM((n1, GROUP), F32)],
        name="fourier_mix",
        compiler_params=_params(("arbitrary", "arbitrary"), 62 * 1024 * 1024),
    )(p, chan, f1, f2, dc, ds)


def _rope_tables(l, n_ctx):
    t = np.arange(l)
    inv = ROPE_BASE ** (-np.arange(QK_ROPE // 4, dtype=np.float64) / (QK_ROPE // 4))
    ang = np.concatenate([(t // GRID_W)[:, None] * inv, (t % GRID_W)[:, None] * inv], axis=-1)
    ang = np.concatenate([ang, np.zeros((n_ctx, QK_ROPE // 2))], axis=0)
    cos, sin = np.cos(ang), np.sin(ang)
    return jnp.asarray(np.concatenate([cos, cos, -sin, sin], axis=-1), F32)


def _rope_apply(r, cs):
    tt = r * cs
    ro = tt + pltpu.roll(tt, QK_ROPE, 1)
    lane = lax.broadcasted_iota(jnp.int32, ro.shape, 1)
    return jnp.where(lane < QK_ROPE, ro, 0.0)


def _q_kernel(cq_ref, g_ref, w_ref, cs_ref, o_ref, n_ref):
    @pl.when(pl.program_id(2) == 0)
    def _():
        n_ref[...] = _rms(cq_ref[0], g_ref[...]).astype(BF16)

    res = jnp.dot(n_ref[...], w_ref[0], preferred_element_type=F32)
    q = jnp.concatenate([res[:, :QK_NOPE], _rope_apply(res[:, QK_NOPE:], cs_ref[...])], axis=1)
    q = q * (math.log2(math.e) * QK_DIM ** -0.5)
    o_ref[0, 0] = q.T.astype(o_ref.dtype)


def _mla_queries(p, l, col, q_g, w_uq, cs):
    b = p.shape[0]
    rank = w_uq.shape[0]
    w = w_uq.reshape(rank, N_GROUPS, QK_DIM)
    x1 = w[:, :, QK_NOPE:QK_NOPE + QK_ROPE // 2]
    x2 = w[:, :, QK_NOPE + QK_ROPE // 2:]
    w = jnp.concatenate([w[:, :, :QK_NOPE], x1, x2, x2, x1], axis=-1)
    w = jnp.transpose(w, (1, 0, 2)).astype(BF16)
    tm = _pick(l, (1024, 512, 256, 128))
    return pl.pallas_call(
        _q_kernel,
        grid=(b, l // tm, N_GROUPS),
        in_specs=[pl.BlockSpec((1, tm, rank), lambda bi, i, h: (bi, i, col)),
                  pl.BlockSpec((1, rank), lambda bi, i, h: (0, 0)),
                  pl.BlockSpec((1, rank, QK_PAD), lambda bi, i, h: (h, 0, 0)),
                  pl.BlockSpec((tm, 128), lambda bi, i, h: (i, 0))],
        out_specs=pl.BlockSpec((1, 1, QK_PAD, tm), lambda bi, i, h: (bi, h, 0, i)),
        out_shape=jax.ShapeDtypeStruct((b, N_GROUPS, QK_PAD, l), BF16),
        scratch_shapes=[pltpu.VMEM((tm, rank), BF16)],
        name="mla_q",
        compiler_params=_params(("arbitrary", "arbitrary", "arbitrary")),
    )(p, q_g.reshape(1, rank), w, cs)


def _kv_kernel(ckv_ref, kr_ref, g_ref, w_ref, cs_ref, k_ref, v_ref, n_ref, rope_ref):
    @pl.when(pl.program_id(2) == 0)
    def _():
        n_ref[...] = _rms(ckv_ref[0], g_ref[...]).astype(BF16)
        rope_ref[...] = _rope_apply(kr_ref[0], cs_ref[...]).astype(BF16)

    res = jnp.dot(n_ref[...], w_ref[...], preferred_element_type=F32)
    k_ref[0, 0] = jnp.concatenate([res[:, :QK_NOPE].astype(BF16), rope_ref[...]], axis=1)
    vt = res[:, QK_NOPE:].T.astype(BF16)
    for c in range(vt.shape[1] // 128):
        v_ref[0, 0, c] = vt[:, c * 128:(c + 1) * 128]


def _mla_keys_values(p, col_kv, col_kr, kv_g, w_ukv, cs):
    b, lk, _ = p.shape
    rank = w_ukv.shape[0]
    tm = _pick(lk, (1280, 640, 256, 128))
    hw = QK_NOPE + V_DIM
    return pl.pallas_call(
        _kv_kernel,
        grid=(b, lk // tm, N_GROUPS),
        in_specs=[pl.BlockSpec((1, tm, rank), lambda bi, i, h: (bi, i, col_kv)),
                  pl.BlockSpec((1, tm, 128), lambda bi, i, h: (bi, i, col_kr)),
                  pl.BlockSpec((1, rank), lambda bi, i, h: (0, 0)),
                  pl.BlockSpec((rank, hw), lambda bi, i, h: (0, h)),
                  pl.BlockSpec((tm, 128), lambda bi, i, h: (i, 0))],
        out_specs=[pl.BlockSpec((1, 1, tm, QK_PAD), lambda bi, i, h: (bi, h, i, 0)),
                   pl.BlockSpec((1, 1, tm // 128, V_DIM, 128), lambda bi, i, h: (bi, h, i, 0, 0))],
        out_shape=[jax.ShapeDtypeStruct((b, N_GROUPS, lk, QK_PAD), BF16),
                   jax.ShapeDtypeStruct((b, N_GROUPS, lk // 128, V_DIM, 128), BF16)],
        scratch_shapes=[pltpu.VMEM((tm, rank), BF16), pltpu.VMEM((tm, 128), BF16)],
        name="mla_kv",
        compiler_params=_params(("arbitrary", "arbitrary", "arbitrary")),
    )(p, p, kv_g.reshape(1, rank), w_ukv.astype(BF16), cs)


def _attn_kernel(qt_ref, k_ref, vt_ref, o_ref, m_ref, l_ref, acc_ref, sa_ref, sb_ref, *, tks, n_sub, rem):
    qt = qt_ref[0, 0]

    def scores(r0, width):
        return jnp.dot(k_ref[0, 0, pl.ds(r0, width), :], qt, preferred_element_type=F32)

    def update(s, c0, width):
        m_prev = m_ref[...]
        m_next = jnp.maximum(m_prev, jnp.max(s, axis=0, keepdims=True))
        p = jnp.exp2(s - pltpu.repeat(m_next, width // 8, axis=0))
        alpha = jnp.exp2(m_prev - m_next)
        l_ref[...] = alpha * l_ref[...] + jnp.sum(p, axis=0, keepdims=True)
        vt = jnp.concatenate([vt_ref[0, 0, c0 + c] for c in range(width // 128)], axis=1)
        pv = jnp.dot(vt, p.astype(BF16), preferred_element_type=F32)
        acc_ref[...] = pltpu.repeat(alpha, V_DIM // 8, axis=0) * acc_ref[...] + pv
        m_ref[...] = m_next

    def start(j):
        return pl.multiple_of(j * tks, tks)

    cpt = tks // 128
    m_ref[...] = jnp.full_like(m_ref, -jnp.inf)
    l_ref[...] = jnp.zeros_like(l_ref)
    acc_ref[...] = jnp.zeros_like(acc_ref)
    sa_ref[...] = scores(0, tks)

    def pair(i, _):
        sb_ref[...] = scores(start(2 * i + 1), tks)
        update(sa_ref[...], 2 * i * cpt, tks)
        sa_ref[...] = scores(start(2 * i + 2), tks)
        update(sb_ref[...], (2 * i + 1) * cpt, tks)
        return 0

    n_pairs = n_sub // 2
    lax.fori_loop(0, n_pairs - 1, pair, 0, unroll=5 if (n_pairs - 1) % 5 == 0 else 1)
    done = 2 * (n_pairs - 1)
    sb_ref[...] = scores(done * tks + tks, tks)
    update(sa_ref[...], done * cpt, tks)
    tail = [(j * tks, tks) for j in range(done + 2, n_sub)] + ([(n_sub * tks, rem)] if rem else [])
    bufs = [sa_ref, sb_ref]
    pending = (sb_ref, (done + 1) * tks, tks)
    for idx, (r0, width) in enumerate(tail):
        nxt = bufs[idx % 2]
        nxt[:width, :] = scores(r0, width)
        update(pending[0][:pending[2], :], pending[1] // 128, pending[2])
        pending = (nxt, r0, width)
    update(pending[0][:pending[2], :], pending[1] // 128, pending[2])
    out_t = acc_ref[...] / pltpu.repeat(l_ref[...], V_DIM // 8, axis=0)
    o_ref[0] = out_t.T.astype(o_ref.dtype)


def _attention(qt, k, vt):
    b, h, _, l = qt.shape
    lk = k.shape[2]
    tq = _pick(l, (1024, 512, 256, 128))
    tks = 512
    n_sub, rem = lk // tks, lk % tks
    assert n_sub >= 2 and rem % 128 == 0
    return pl.pallas_call(
        functools.partial(_attn_kernel, tks=tks, n_sub=n_sub, rem=rem),
        grid=(b, h, l // tq),
        in_specs=[pl.BlockSpec((1, 1, QK_PAD, tq), lambda bi, hi, i: (bi, hi, 0, i)),
                  pl.BlockSpec((1, 1, lk, QK_PAD), lambda bi, hi, i: (bi, hi, 0, 0)),
                  pl.BlockSpec((1, 1, lk // 128, V_DIM, 128), lambda bi, hi, i: (bi, hi, 0, 0, 0))],
        out_specs=pl.BlockSpec((1, tq, V_DIM), lambda bi, hi, i: (bi, i, hi)),
        out_shape=jax.ShapeDtypeStruct((b, l, h * V_DIM), BF16),
        scratch_shapes=[pltpu.VMEM((8, tq), F32), pltpu.VMEM((8, tq), F32), pltpu.VMEM((V_DIM, tq), F32),
                        pltpu.VMEM((tks, tq), F32), pltpu.VMEM((tks, tq), F32)],
        name="flash_attention",
        compiler_params=_params(("arbitrary", "arbitrary", "arbitrary")),
    )(qt, k, vt)


def _conv_lru_layer(xc, xl, ml, mc, ng, w1, w2, layer, w_in, w_out, cv_w, cv_b, cv_g, cv_beta,
                    rg_conv_w, rg_conv_b, rg_wa, rg_ba, rg_wi, rg_bi, rg_lambda, need_ctx):
    w_in = w_in.astype(BF16)
    w_out = w_out.astype(BF16)
    pc = _norm_matmul(xc, mc, ng[0], w_in, 0, 1)
    plat = _norm_matmul(xl, ml, ng[0], w_in, 0, 1)
    c = cv_w.shape[1]
    zero = jnp.zeros((xl.shape[0], 8, c), F32)
    hs_c, hs_l = [], []
    for d, reverse in enumerate((False, True)):
        args = (rg_conv_w[d], rg_conv_b[d], rg_wa[d], rg_ba[d], rg_wi[d], rg_bi[d], rg_lambda[d], reverse)
        h_c, last_c = _rglru(pc, 2, zero, *args)
        h_l, _ = _rglru(plat, 2, last_c, *args)
        hs_c.append(h_c)
        hs_l.append(h_l)

    def finish(p, hs, x, mod):
        conv_out = _conformer_conv(p, cv_w, cv_b, cv_g, cv_beta)
        x = _mixout_ab(conv_out, hs[0], hs[1], p, x, mod, ng[1], w_out)
        return _mlp(x, mod, ng[2], ng[3], w1, w2, layer)

    xl = finish(plat, hs_l, xl, ml)
    if need_ctx:
        xc = finish(pc, hs_c, xc, mc)
    return xc, xl


def _fourier_mla_layer(xc, xl, ml, mc, ng, w1, w2, layer, w_in, w_out, q_g, kv_g, w_uq, w_ukv):
    b, l, d = xl.shape
    n_ctx = xc.shape[1]
    c = N_GROUPS * GROUP
    q_rank = q_g.shape[0]
    kv_rank = kv_g.shape[0]
    kr = w_in[:, c + q_rank + kv_rank:]
    kr1, kr2 = kr[:, :QK_ROPE // 2], kr[:, QK_ROPE // 2:]
    n_used = c + q_rank + kv_rank + 2 * QK_ROPE
    n_pad = -n_used % 256
    w_in_p = jnp.concatenate([w_in[:, :c + q_rank + kv_rank], kr1, kr2, kr2, kr1,
                              jnp.zeros((d, n_pad), w_in.dtype)], axis=1).astype(BF16)
    lk = l + n_ctx
    p = _norm_matmul(xl, ml, ng[0], w_in_p, 0, 1, out_rows=lk)
    tmc = _pick(n_ctx, (1024, 512, 256, 128))
    p = _norm_matmul(xc, mc, ng[0], w_in_p, 0, 1, out_rows=lk, row_block_off=l // tmc, into=p)
    cs = _rope_tables(l, n_ctx)
    four = _fourier_mix(p, l)
    q = _mla_queries(p, l, c // q_rank, q_g, w_uq, cs)
    k, v = _mla_keys_values(p, (c + q_rank) // kv_rank, (c + q_rank + kv_rank) // 128, kv_g, w_ukv, cs)
    att = _attention(q, k, v)
    xl = _mixout_cd(four, att, xl, ml, ng[1], w_out.astype(BF16))
    return _mlp(xl, ml, ng[2], ng[3], w1, w2, layer)


def kernel(x, c, ctx, c_ctx, mod_w, mod_b, norm_g, mlp_w1, mlp_w2, ab_w_in, ab_w_out, cv_w, cv_b, cv_norm_g, cv_norm_b, rg_conv_w, rg_conv_b, rg_wa, rg_ba, rg_wi, rg_bi, rg_lambda, cd_w_in, cd_w_out, mla_q_norm_g, mla_kv_norm_g, mla_w_uq, mla_w_ukv):
    b, l, d = x.shape
    depth = mod_w.shape[0]
    assert b + 1 <= 8 and l % GRID_W == 0
    cond8 = jnp.concatenate([c, c_ctx[None, :], jnp.zeros((8 - b - 1, d), F32)], axis=0)
    mods = _modulation(cond8, mod_w, mod_b)
    w1, w2 = mlp_w1.astype(BF16), mlp_w2.astype(BF16)
    xc, xl = ctx, x
    for i in range(depth):
        need_ctx = i < depth - 1
        ml = mods[i, :b].reshape(b, 6, d)
        mc = jnp.broadcast_to(mods[i, b].reshape(1, 6, d), (b, 6, d))
        j = i // 2
        if i % 2 == 0:
            xc, xl = _conv_lru_layer(xc, xl, ml, mc, norm_g[i], w1, w2, i, ab_w_in[j], ab_w_out[j],
                                     cv_w[j], cv_b[j], cv_norm_g[j], cv_norm_b[j], rg_conv_w[j], rg_conv_b[j],
                                     rg_wa[j], rg_ba[j], rg_wi[j], rg_bi[j], rg_lambda[j], need_ctx)
        else:
            xl = _fourier_mla_layer(xc, xl, ml, mc, norm_g[i], w1, w2, i, cd_w_in[j], cd_w_out[j],
                                    mla_q_norm_g[j], mla_kv_norm_g[j], mla_w_uq[j], mla_w_ukv[j])
    return xl
```

```python
import functools
import math

import numpy as np
import jax
import jax.numpy as jnp
from jax import lax
from jax.experimental import pallas as pl
from jax.experimental.pallas import tpu as pltpu

F32 = jnp.float32
BF16 = jnp.bfloat16

EPS = 1e-6
GRID_W = 64
N_GROUPS = 8
GROUP = 128
A_CONV = 31
RG_CONV = 4
RG_C = 8.0
QK_NOPE = 128
QK_ROPE = 64
QK_DIM = QK_NOPE + QK_ROPE
V_DIM = 128
ROPE_BASE = 10000.0
QK_PAD = 256
CONV_HALO = 16
LRU_HALO = 8
FFT_GROUP = 8
VMEM_LIMIT = 56 * 1024 * 1024


def _params(sem, vmem=VMEM_LIMIT):
    return pltpu.CompilerParams(dimension_semantics=sem, vmem_limit_bytes=vmem)


def _pick(n, candidates):
    for c in candidates:
        if n % c == 0:
            return c
    raise ValueError(f"no tile for {n} in {candidates}")


def _rms(x, g):
    return x * lax.rsqrt(jnp.mean(x * x, axis=-1, keepdims=True) + EPS) * g


ROW_CHUNK = 16


def _for_row_chunks(n_rows, body, unroll):
    def step(i, _):
        body(pl.multiple_of(i * ROW_CHUNK, ROW_CHUNK))
        return 0
    lax.fori_loop(0, n_rows // ROW_CHUNK, step, 0, unroll=unroll)


def _row_rsqrt_ms(src, r_ref):
    def body(r0):
        rows = pl.ds(r0, ROW_CHUNK)
        x = src(rows)
        r = lax.rsqrt(jnp.mean(x * x, axis=-1, keepdims=True) + EPS)
        r_ref[rows, :] = jnp.broadcast_to(r, (ROW_CHUNK, r_ref.shape[1]))

    _for_row_chunks(r_ref.shape[0], body, 8)


def _norm_modulate_rows(x_ref, g_ref, m_ref, shift_row, scale_row, h_ref, r_ref):
    _row_rsqrt_ms(lambda rows: x_ref[0, rows, :], r_ref)
    gain = g_ref[...] * (1.0 + m_ref[0, scale_row:scale_row + 1, :])
    shift = m_ref[0, shift_row:shift_row + 1, :]
    reps = h_ref.shape[1] // r_ref.shape[1]

    def body(r0):
        rows = pl.ds(r0, ROW_CHUNK)
        r = pltpu.repeat(r_ref[rows, :], reps, axis=1)
        h_ref[rows, :] = (x_ref[0, rows, :] * r * gain + shift).astype(h_ref.dtype)

    _for_row_chunks(h_ref.shape[0], body, 4)


def _mod_kernel(c_ref, w_ref, b_ref, o_ref):
    c = c_ref[...]
    s = c * jax.nn.sigmoid(c)
    o_ref[0] = jnp.dot(s, w_ref[0], preferred_element_type=F32,
                       precision=lax.Precision.HIGHEST) + b_ref[0]


def _modulation(cond8, mod_w, mod_b):
    depth, d, n = mod_w.shape
    tn = _pick(n, (1024, 512, 256, 128))
    return pl.pallas_call(
        _mod_kernel,
        grid=(depth, n // tn),
        in_specs=[pl.BlockSpec((8, d), lambda l, j: (0, 0)),
                  pl.BlockSpec((1, d, tn), lambda l, j: (l, 0, j)),
                  pl.BlockSpec((1, 1, tn), lambda l, j: (l, 0, j))],
        out_specs=pl.BlockSpec((1, 8, tn), lambda l, j: (l, 0, j)),
        out_shape=jax.ShapeDtypeStruct((depth, 8, n), F32),
        name="modulation",
        compiler_params=_params(("arbitrary", "arbitrary")),
    )(cond8, mod_w, mod_b.reshape(depth, 1, n))


def _norm_matmul_kernel(*refs, shift_row, scale_row, aliased):
    if aliased:
        x_ref, m_ref, g_ref, w_ref, _, o_ref, h_ref, r_ref = refs
    else:
        x_ref, m_ref, g_ref, w_ref, o_ref, h_ref, r_ref = refs

    @pl.when(pl.program_id(2) == 0)
    def _():
        _norm_modulate_rows(x_ref, g_ref, m_ref, shift_row, scale_row, h_ref, r_ref)

    o_ref[0] = jnp.dot(h_ref[...], w_ref[...], preferred_element_type=F32).astype(o_ref.dtype)


def _norm_matmul(x, mod, g, w, shift_row, scale_row, out_rows=None, row_block_off=0, into=None):
    b, l, d = x.shape
    n = w.shape[1]
    tm = _pick(l, (1024, 512, 256, 128))
    tn = _pick(n, (1024, 768, 512, 256, 128))
    out_rows = l if out_rows is None else out_rows
    off = row_block_off
    in_specs = [pl.BlockSpec((1, tm, d), lambda bi, i, j: (bi, i, 0)),
                pl.BlockSpec((1, 6, d), lambda bi, i, j: (bi, 0, 0)),
                pl.BlockSpec((1, d), lambda bi, i, j: (0, 0)),
                pl.BlockSpec((d, tn), lambda bi, i, j: (0, j))]
    args = [x, mod, g.reshape(1, d), w]
    aliases = {}
    if into is not None:
        in_specs.append(pl.BlockSpec(memory_space=pl.ANY))
        args.append(into)
        aliases = {4: 0}
    return pl.pallas_call(
        functools.partial(_norm_matmul_kernel, shift_row=shift_row, scale_row=scale_row,
                          aliased=into is not None),
        grid=(b, l // tm, n // tn),
        in_specs=in_specs,
        out_specs=pl.BlockSpec((1, tm, tn), lambda bi, i, j: (bi, i + off, j)),
        out_shape=jax.ShapeDtypeStruct((b, out_rows, n), F32),
        scratch_shapes=[pltpu.VMEM((tm, d), BF16), pltpu.VMEM((tm, 128), F32)],
        input_output_aliases=aliases,
        name="norm_matmul",
        compiler_params=_params(("arbitrary", "arbitrary", "arbitrary")),
    )(*args)


def _conv_kernel(v_ref, gt_ref, vp_ref, gp_ref, vn_ref, gn_ref, w_ref, b_ref, lg_ref, lb_ref,
                 o_ref, ubuf, sh_ref, *, t, rc):
    i = pl.program_id(1)
    first = i == 0
    last = i == pl.num_programs(1) - 1
    halo = CONV_HALO
    ubuf[halo:halo + t, :] = v_ref[0] * jax.nn.sigmoid(gt_ref[0])
    up = vp_ref[0] * jax.nn.sigmoid(gp_ref[0])
    un = vn_ref[0] * jax.nn.sigmoid(gn_ref[0])
    ubuf[0:halo, :] = jnp.where(first, 0.0, up)
    ubuf[halo + t:2 * halo + t, :] = jnp.where(last, 0.0, un)
    base = halo - A_CONV // 2
    span = t + 8 * ((base + A_CONV - 1) // 8)
    for s in range(1, 8):
        sh_ref[s - 1] = ubuf[s:s + span, :]
    for r0 in range(0, t, rc):
        for g in range(N_GROUPS):
            cs = slice(g * GROUP, (g + 1) * GROUP)
            acc = jnp.zeros((rc, GROUP), F32)
            for k in range(A_CONV):
                s, a8 = (base + k) % 8, 8 * ((base + k) // 8)
                src = ubuf if s == 0 else sh_ref.at[s - 1]
                acc = acc + w_ref[k:k + 1, cs] * src[r0 + a8:r0 + a8 + rc, cs]
            y = acc + b_ref[:, cs]
            mu = jnp.mean(y, axis=-1, keepdims=True)
            dlt = y - mu
            yn = dlt * lax.rsqrt(jnp.mean(dlt * dlt, axis=-1, keepdims=True) + EPS)
            z = yn * lg_ref[:, cs] + lb_ref[:, cs]
            o_ref[0, r0:r0 + rc, cs] = (z * jax.nn.sigmoid(z)).astype(o_ref.dtype)


def _conformer_conv(p, cv_w, cv_b, ln_g, ln_b):
    b, l, _ = p.shape
    c = cv_w.shape[1]
    t = _pick(l, (256, 128))
    rc = 64
    hb = t // CONV_HALO
    nhb = l // CONV_HALO
    main = lambda col: pl.BlockSpec((1, t, c), lambda bi, i: (bi, i, col))
    prev = lambda col: pl.BlockSpec((1, CONV_HALO, c), lambda bi, i: (bi, jnp.maximum(i * hb - 1, 0), col))
    nxt = lambda col: pl.BlockSpec((1, CONV_HALO, c), lambda bi, i: (bi, jnp.minimum((i + 1) * hb, nhb - 1), col))
    vec = pl.BlockSpec((1, c), lambda bi, i: (0, 0))
    return pl.pallas_call(
        functools.partial(_conv_kernel, t=t, rc=rc),
        grid=(b, l // t),
        in_specs=[main(0), main(1), prev(0), prev(1), nxt(0), nxt(1),
                  pl.BlockSpec((A_CONV, c), lambda bi, i: (0, 0)), vec, vec, vec],
        out_specs=pl.BlockSpec((1, t, c), lambda bi, i: (bi, i, 0)),
        out_shape=jax.ShapeDtypeStruct((b, l, c), BF16),
        scratch_shapes=[pltpu.VMEM((t + 2 * CONV_HALO, c), F32),
                        pltpu.VMEM((7, t + 8 * ((CONV_HALO + A_CONV // 2) // 8), c), F32)],
        name="conformer_conv",
        compiler_params=_params(("arbitrary", "arbitrary")),
    )(p, p, p, p, p, p, cv_w, cv_b.reshape(1, c), ln_g.reshape(1, c), ln_b.reshape(1, c))


def _lru_kernel(x_ref, xh_ref, cw_ref, cb_ref, wg_ref, ba_ref, bi_ref, lam_ref, h0_ref,
                h_ref, hl_ref, xbuf, a_buf, b_buf, carry, *, t, reverse):
    i = pl.program_id(1)
    c = x_ref.shape[-1]
    halo = LRU_HALO

    @pl.when(i == 0)
    def _():
        carry[...] = h0_ref[0]

    edge = jnp.where(i == 0, 0.0, xh_ref[0])
    if reverse:
        xbuf[0:t, :] = x_ref[0]
        xbuf[t:t + halo, :] = edge
        base = 0
    else:
        xbuf[0:halo, :] = edge
        xbuf[halo:halo + t, :] = x_ref[0]
        base = halo - (RG_CONV - 1)
    xall = xbuf[...]
    n = t + halo
    y = jnp.zeros((t, c), F32) + cb_ref[...]
    for k in range(RG_CONV):
        win = xall if base + k == 0 else pltpu.roll(xall, n - (base + k), 0)
        y = y + cw_ref[k:k + 1, :] * win[:t]

    log_sig = jax.nn.log_sigmoid(lam_ref[...])
    for h in range(N_GROUPS):
        cs = slice(h * GROUP, (h + 1) * GROUP)
        yh = y[:, cs]
        gates = jnp.dot(yh.astype(BF16), wg_ref[h], preferred_element_type=F32)
        r = jax.nn.sigmoid(gates[:, :GROUP] + ba_ref[:, cs])
        ig = jax.nn.sigmoid(gates[:, GROUP:] + bi_ref[:, cs])
        log_a = RG_C * r * log_sig[:, cs]
        a = jnp.exp(log_a)
        a_buf[:, cs] = a
        b_buf[:, cs] = jnp.sqrt(1.0 - a * a) * (ig * yh)

    ngroups = t // 8
    row = lax.broadcasted_iota(jnp.int32, (8, c), 0)

    def body(gidx, cr):
        gi = (ngroups - 1 - gidx) if reverse else gidx
        r0 = pl.multiple_of(gi * 8, 8)
        a = a_buf[pl.ds(r0, 8), :]
        bb = b_buf[pl.ds(r0, 8), :]
        for k in (1, 2, 4):
            if reverse:
                a_s = pltpu.roll(a, 8 - k, 0)
                b_s = pltpu.roll(bb, 8 - k, 0)
                valid = row < 8 - k
            else:
                a_s = pltpu.roll(a, k, 0)
                b_s = pltpu.roll(bb, k, 0)
                valid = row >= k
            bb = jnp.where(valid, a * b_s + bb, bb)
            a = jnp.where(valid, a * a_s, a)
        hh = bb + a * cr
        h_ref[0, pl.ds(r0, 8), :] = hh
        edge_row = hh[0:1, :] if reverse else hh[7:8, :]
        return jnp.broadcast_to(edge_row, (8, c))

    cr = lax.fori_loop(0, ngroups, body, carry[...], unroll=4)
    carry[...] = cr
    hl_ref[0] = cr


def _rglru(p, col, h0, conv_w, conv_b, wa, ba, wi, bi, lam, reverse):
    b, l, _ = p.shape
    c = conv_w.shape[1]
    t = _pick(l, (256, 128))
    nt = l // t
    hb = t // LRU_HALO
    nhb = l // LRU_HALO
    if reverse:
        tile = lambda bi_, i: (bi_, nt - 1 - i, col)
        halo = lambda bi_, i: (bi_, jnp.minimum((nt - i) * hb, nhb - 1), col)
        otile = lambda bi_, i: (bi_, nt - 1 - i, 0)
    else:
        tile = lambda bi_, i: (bi_, i, col)
        halo = lambda bi_, i: (bi_, jnp.maximum(i * hb - 1, 0), col)
        otile = lambda bi_, i: (bi_, i, 0)
    wg = jnp.concatenate([wa, wi], axis=-1).astype(BF16)
    vec = pl.BlockSpec((1, c), lambda bi_, i: (0, 0))
    state = pl.BlockSpec((1, 8, c), lambda bi_, i: (bi_, 0, 0))
    return pl.pallas_call(
        functools.partial(_lru_kernel, t=t, reverse=reverse),
        grid=(b, nt),
        in_specs=[pl.BlockSpec((1, t, c), tile),
                  pl.BlockSpec((1, LRU_HALO, c), halo),
                  pl.BlockSpec((RG_CONV, c), lambda bi_, i: (0, 0)), vec,
                  pl.BlockSpec((N_GROUPS, GROUP, 2 * GROUP), lambda bi_, i: (0, 0, 0)),
                  vec, vec, vec, state],
        out_specs=[pl.BlockSpec((1, t, c), otile), state],
        out_shape=[jax.ShapeDtypeStruct((b, l, c), F32), jax.ShapeDtypeStruct((b, 8, c), F32)],
        scratch_shapes=[pltpu.VMEM((t + LRU_HALO, c), F32), pltpu.VMEM((t, c), F32),
                        pltpu.VMEM((t, c), F32), pltpu.VMEM((8, c), F32)],
        name="rglru_rev" if reverse else "rglru_fwd",
        compiler_params=_params(("arbitrary", "arbitrary")),
    )(p, p, conv_w, conv_b.reshape(1, c), wg, ba.reshape(1, c), bi.reshape(1, c), lam.reshape(1, c), h0)


def _mixout_ab_kernel(cv_ref, hf_ref, hr_ref, pg_ref, x_ref, m_ref, g_ref, wa_ref, wb_ref, o_ref):
    rec = (hf_ref[0] + hr_ref[0]) * jax.nn.gelu(pg_ref[0])
    y = jnp.dot(cv_ref[0], wa_ref[...], preferred_element_type=F32)
    y = y + jnp.dot(rec.astype(BF16), wb_ref[...], preferred_element_type=F32)
    o_ref[0] = x_ref[0] + m_ref[0, 2:3, :] * _rms(y, g_ref[...])


def _mixout_ab(conv_out, h_f, h_r, p, x, mod, g, w_out):
    b, l, d = x.shape
    c = conv_out.shape[-1]
    tm = _pick(l, (512, 256, 128))
    half = lambda: pl.BlockSpec((1, tm, c), lambda bi, i: (bi, i, 0))
    wspec = lambda k: pl.BlockSpec((c, d), lambda bi, i: (k, 0))
    return pl.pallas_call(
        _mixout_ab_kernel,
        grid=(b, l // tm),
        in_specs=[half(), half(), half(),
                  pl.BlockSpec((1, tm, c), lambda bi, i: (bi, i, 3)),
                  pl.BlockSpec((1, tm, d), lambda bi, i: (bi, i, 0)),
                  pl.BlockSpec((1, 6, d), lambda bi, i: (bi, 0, 0)),
                  pl.BlockSpec((1, d), lambda bi, i: (0, 0)),
                  wspec(0), wspec(1)],
        out_specs=pl.BlockSpec((1, tm, d), lambda bi, i: (bi, i, 0)),
        out_shape=jax.ShapeDtypeStruct((b, l, d), F32),
        name="mixout_ab",
        compiler_params=_params(("arbitrary", "arbitrary")),
    )(conv_out, h_f, h_r, p, x, mod, g.reshape(1, d), w_out, w_out)


def _mixout_cd_kernel(f_ref, o_att_ref, x_ref, m_ref, g_ref, wa_ref, wb_ref, o_ref):
    y = jnp.dot(f_ref[0].astype(BF16), wa_ref[...], preferred_element_type=F32)
    y = y + jnp.dot(o_att_ref[0], wb_ref[...], preferred_element_type=F32)
    o_ref[0] = x_ref[0] + m_ref[0, 2:3, :] * _rms(y, g_ref[...])


def _mixout_cd(four, att, x, mod, g, w_out):
    b, l, d = x.shape
    c = four.shape[-1]
    tm = _pick(l, (512, 256, 128))
    half = lambda: pl.BlockSpec((1, tm, c), lambda bi, i: (bi, i, 0))
    wspec = lambda k: pl.BlockSpec((c, d), lambda bi, i: (k, 0))
    return pl.pallas_call(
        _mixout_cd_kernel,
        grid=(b, l // tm),
        in_specs=[half(), half(),
                  pl.BlockSpec((1, tm, d), lambda bi, i: (bi, i, 0)),
                  pl.BlockSpec((1, 6, d), lambda bi, i: (bi, 0, 0)),
                  pl.BlockSpec((1, d), lambda bi, i: (0, 0)),
                  wspec(0), wspec(1)],
        out_specs=pl.BlockSpec((1, tm, d), lambda bi, i: (bi, i, 0)),
        out_shape=jax.ShapeDtypeStruct((b, l, d), F32),
        name="mixout_cd",
        compiler_params=_params(("arbitrary", "arbitrary")),
    )(four, att, x, mod, g.reshape(1, d), w_out, w_out)


def _mlp_kernel(x_ref, m_ref, gpre_ref, gpost_ref, w1_ref, w2_ref, o_ref, h_ref, acc_ref, r_ref):
    j = pl.program_id(2)

    @pl.when(j == 0)
    def _():
        _norm_modulate_rows(x_ref, gpre_ref, m_ref, 3, 4, h_ref, r_ref)
        acc_ref[...] = jnp.zeros_like(acc_ref)

    u = jnp.dot(h_ref[...], w1_ref[...], preferred_element_type=F32)
    u = jnp.square(jnp.maximum(u, 0.0))
    acc_ref[...] += jnp.dot(u.astype(BF16), w2_ref[...], preferred_element_type=F32)

    @pl.when(j == pl.num_programs(2) - 1)
    def _():
        _row_rsqrt_ms(lambda rows: acc_ref[rows, :], r_ref)
        gain = m_ref[0, 5:6, :] * gpost_ref[...]
        reps = acc_ref.shape[1] // r_ref.shape[1]

        def body(r0):
            rows = pl.ds(r0, ROW_CHUNK)
            r = pltpu.repeat(r_ref[rows, :], reps, axis=1)
            o_ref[0, rows, :] = x_ref[0, rows, :] + acc_ref[rows, :] * r * gain

        _for_row_chunks(acc_ref.shape[0], body, 4)


def _mlp(x, mod, g_pre, g_post, w1, w2, layer):
    b, l, d = x.shape
    f = w1.shape[2]
    tm = _pick(l, (512, 256, 128))
    tf = _pick(f, (1024, 512, 256, 128))
    return pl.pallas_call(
        _mlp_kernel,
        grid=(b, l // tm, f // tf),
        in_specs=[pl.BlockSpec((1, tm, d), lambda bi, i, j: (bi, i, 0)),
                  pl.BlockSpec((1, 6, d), lambda bi, i, j: (bi, 0, 0)),
                  pl.BlockSpec((1, d), lambda bi, i, j: (0, 0)),
                  pl.BlockSpec((1, d), lambda bi, i, j: (0, 0)),
                  pl.BlockSpec((None, d, tf), lambda bi, i, j: (layer, 0, j)),
                  pl.BlockSpec((None, tf, d), lambda bi, i, j: (layer, j, 0))],
        out_specs=pl.BlockSpec((1, tm, d), lambda bi, i, j: (bi, i, 0)),
        out_shape=jax.ShapeDtypeStruct((b, l, d), F32),
        scratch_shapes=[pltpu.VMEM((tm, d), BF16), pltpu.VMEM((tm, d), F32), pltpu.VMEM((tm, 128), F32)],
        name="mlp",
        compiler_params=_params(("arbitrary", "arbitrary", "arbitrary")),
    )(x, mod, g_pre.reshape(1, d), g_post.reshape(1, d), w1, w2)


def _fourier_tables(l):
    n1 = 1 << (int(math.log2(l)) // 2)
    while l % n1:
        n1 //= 2
    n2 = l // n1
    def cs(n):
        idx = np.arange(n)
        ang = 2.0 * np.pi * ((idx[:, None] * idx[None, :]) % n) / n
        return np.cos(ang), np.sin(ang)
    c1, s1 = cs(n1)
    c2, s2 = cs(n2)
    cc, sc = cs(GROUP)
    delta = 2.0 * np.pi * np.arange(n1)[:, None] / l * np.ones((1, GROUP))
    norm = 1.0 / math.sqrt(l * GROUP)
    chan = np.concatenate([cc, -sc], axis=1)
    f1 = np.concatenate([c1, s1], axis=0)
    f2 = np.concatenate([c2, s2], axis=1) * norm
    return (n1, n2, jnp.asarray(chan, BF16), jnp.asarray(f1, BF16), jnp.asarray(f2, BF16),
            jnp.asarray(np.cos(delta), F32), jnp.asarray(np.sin(delta), F32))


def _fourier_kernel(x_ref, chan_ref, f1_ref, f2_ref, dc_ref, ds_ref, o_ref, are_buf, aim_buf, tw_c, tw_s,
                    *, n1, n2):
    tw_c[...] = jnp.ones_like(tw_c)
    tw_s[...] = jnp.zeros_like(tw_s)

    def stage1(jg, _):
        j0 = jg * FFT_GROUP
        xs = jnp.concatenate([x_ref[0, pl.ds(j0 + u, n1, stride=n2), :] for u in range(FFT_GROUP)], axis=0)
        pq = jnp.dot(xs.astype(BF16), chan_ref[...], preferred_element_type=F32)
        pq = jnp.concatenate([pq[u * n1:(u + 1) * n1] for u in range(FFT_GROUP)], axis=1).astype(BF16)
        r = jnp.dot(f1_ref[...], pq, preferred_element_type=F32)
        for u in range(FFT_GROUP):
            ru = r[:, u * 2 * GROUP:(u + 1) * 2 * GROUP]
            a_re = ru[:n1, :GROUP] + ru[n1:, GROUP:]
            a_im = ru[:n1, GROUP:] - ru[n1:, :GROUP]
            tc = tw_c[...]
            ts = tw_s[...]
            row0 = pl.multiple_of((j0 + u) * n1, 8)
            are_buf[pl.ds(row0, n1), :] = a_re * tc + a_im * ts
            aim_buf[pl.ds(row0, n1), :] = a_im * tc - a_re * ts
            tw_c[...] = tc * dc_ref[...] - ts * ds_ref[...]
            tw_s[...] = ts * dc_ref[...] + tc * ds_ref[...]
        return 0

    lax.fori_loop(0, n2 // FFT_GROUP, stage1, 0)

    def stage2(kg, _):
        k0 = kg * FFT_GROUP
        rhs = jnp.concatenate(
            [jnp.concatenate([are_buf[pl.ds(k0 + u, n2, stride=n1), :], aim_buf[pl.ds(k0 + u, n2, stride=n1), :]],
                             axis=0) for u in range(FFT_GROUP)], axis=1).astype(BF16)
        y = jnp.dot(f2_ref[...], rhs, preferred_element_type=F32)
        for u in range(FFT_GROUP):
            o_ref[0, pl.ds(k0 + u, n2, stride=n1), :] = y[:, u * GROUP:(u + 1) * GROUP]
        return 0

    lax.fori_loop(0, n1 // FFT_GROUP, stage2, 0)


def _fourier_mix(p, l):
    b = p.shape[0]
    n1, n2, chan, f1, f2, dc, ds = _fourier_tables(l)
    const = lambda a: pl.BlockSpec(a.shape, lambda bi, g: (0, 0))
    return pl.pallas_call(
        functools.partial(_fourier_kernel, n1=n1, n2=n2),
        grid=(b, N_GROUPS),
        in_specs=[pl.BlockSpec((1, l, GROUP), lambda bi, g: (bi, 0, g)),
                  const(chan), const(f1), const(f2), const(dc), const(ds)],
        out_specs=pl.BlockSpec((1, l, GROUP), lambda bi, g: (bi, 0, g)),
        out_shape=jax.ShapeDtypeStruct((b, l, N_GROUPS * GROUP), F32),
        scratch_shapes=[pltpu.VMEM((l, GROUP), F32), pltpu.VMEM((l, GROUP), F32),
                        pltpu.VMEM((n1, GROUP), F32), pltpu.VMEM((n1, GROUP), F32)],
        name="fourier_mix",
        compiler_params=_params(("arbitrary", "arbitrary"), 62 * 1024 * 1024),
    )(p, chan, f1, f2, dc, ds)


def _rope_tables(l, n_ctx):
    t = np.arange(l)
    inv = ROPE_BASE ** (-np.arange(QK_ROPE // 4, dtype=np.float64) / (QK_ROPE // 4))
    ang = np.concatenate([(t // GRID_W)[:, None] * inv, (t % GRID_W)[:, None] * inv], axis=-1)
    ang = np.concatenate([ang, np.zeros((n_ctx, QK_ROPE // 2))], axis=0)
    cos, sin = np.cos(ang), np.sin(ang)
    return jnp.asarray(np.concatenate([cos, cos, -sin, sin], axis=-1), F32)


def _rope_apply(r, cs):
    tt = r * cs
    ro = tt + pltpu.roll(tt, QK_ROPE, 1)
    lane = lax.broadcasted_iota(jnp.int32, ro.shape, 1)
    return jnp.where(lane < QK_ROPE, ro, 0.0)


def _q_kernel(cq_ref, g_ref, w_ref, cs_ref, o_ref, n_ref):
    @pl.when(pl.program_id(2) == 0)
    def _():
        n_ref[...] = _rms(cq_ref[0], g_ref[...]).astype(BF16)

    res = jnp.dot(n_ref[...], w_ref[0], preferred_element_type=F32)
    q = jnp.concatenate([res[:, :QK_NOPE], _rope_apply(res[:, QK_NOPE:], cs_ref[...])], axis=1)
    q = q * (math.log2(math.e) * QK_DIM ** -0.5)
    o_ref[0, 0] = q.T.astype(o_ref.dtype)


def _mla_queries(p, l, col, q_g, w_uq, cs):
    b = p.shape[0]
    rank = w_uq.shape[0]
    w = w_uq.reshape(rank, N_GROUPS, QK_DIM)
    x1 = w[:, :, QK_NOPE:QK_NOPE + QK_ROPE // 2]
    x2 = w[:, :, QK_NOPE + QK_ROPE // 2:]
    w = jnp.concatenate([w[:, :, :QK_NOPE], x1, x2, x2, x1], axis=-1)
    w = jnp.transpose(w, (1, 0, 2)).astype(BF16)
    tm = _pick(l, (1024, 512, 256, 128))
    return pl.pallas_call(
        _q_kernel,
        grid=(b, l // tm, N_GROUPS),
        in_specs=[pl.BlockSpec((1, tm, rank), lambda bi, i, h: (bi, i, col)),
                  pl.BlockSpec((1, rank), lambda bi, i, h: (0, 0)),
                  pl.BlockSpec((1, rank, QK_PAD), lambda bi, i, h: (h, 0, 0)),
                  pl.BlockSpec((tm, 128), lambda bi, i, h: (i, 0))],
        out_specs=pl.BlockSpec((1, 1, QK_PAD, tm), lambda bi, i, h: (bi, h, 0, i)),
        out_shape=jax.ShapeDtypeStruct((b, N_GROUPS, QK_PAD, l), BF16),
        scratch_shapes=[pltpu.VMEM((tm, rank), BF16)],
        name="mla_q",
        compiler_params=_params(("arbitrary", "arbitrary", "arbitrary")),
    )(p, q_g.reshape(1, rank), w, cs)


def _kv_kernel(ckv_ref, kr_ref, g_ref, w_ref, cs_ref, k_ref, v_ref, n_ref, rope_ref):
    @pl.when(pl.program_id(2) == 0)
    def _():
        n_ref[...] = _rms(ckv_ref[0], g_ref[...]).astype(BF16)
        rope_ref[...] = _rope_apply(kr_ref[0], cs_ref[...]).astype(BF16)

    res = jnp.dot(n_ref[...], w_ref[...], preferred_element_type=F32)
    k_ref[0, 0] = jnp.concatenate([res[:, :QK_NOPE].astype(BF16), rope_ref[...]], axis=1)
    vt = res[:, QK_NOPE:].T.astype(BF16)
    for c in range(vt.shape[1] // 128):
        v_ref[0, 0, c] = vt[:, c * 128:(c + 1) * 128]


def _mla_keys_values(p, col_kv, col_kr, kv_g, w_ukv, cs):
    b, lk, _ = p.shape
    rank = w_ukv.shape[0]
    tm = _pick(lk, (1280, 640, 256, 128))
    hw = QK_NOPE + V_DIM
    return pl.pallas_call(
        _kv_kernel,
        grid=(b, lk // tm, N_GROUPS),
        in_specs=[pl.BlockSpec((1, tm, rank), lambda bi, i, h: (bi, i, col_kv)),
                  pl.BlockSpec((1, tm, 128), lambda bi, i, h: (bi, i, col_kr)),
                  pl.BlockSpec((1, rank), lambda bi, i, h: (0, 0)),
                  pl.BlockSpec((rank, hw), lambda bi, i, h: (0, h)),
                  pl.BlockSpec((tm, 128), lambda bi, i, h: (i, 0))],
        out_specs=[pl.BlockSpec((1, 1, tm, QK_PAD), lambda bi, i, h: (bi, h, i, 0)),
                   pl.BlockSpec((1, 1, tm // 128, V_DIM, 128), lambda bi, i, h: (bi, h, i, 0, 0))],
        out_shape=[jax.ShapeDtypeStruct((b, N_GROUPS, lk, QK_PAD), BF16),
                   jax.ShapeDtypeStruct((b, N_GROUPS, lk // 128, V_DIM, 128), BF16)],
        scratch_shapes=[pltpu.VMEM((tm, rank), BF16), pltpu.VMEM((tm, 128), BF16)],
        name="mla_kv",
        compiler_params=_params(("arbitrary", "arbitrary", "arbitrary")),
    )(p, p, kv_g.reshape(1, rank), w_ukv.astype(BF16), cs)


def _attn_kernel(qt_ref, k_ref, vt_ref, o_ref, m_ref, l_ref, acc_ref, sa_ref, sb_ref, *, tks, n_sub, rem):
    qt = qt_ref[0, 0]

    def scores(r0, width):
        return jnp.dot(k_ref[0, 0, pl.ds(r0, width), :], qt, preferred_element_type=F32)

    def update(s, c0, width):
        m_prev = m_ref[...]
        m_next = jnp.maximum(m_prev, jnp.max(s, axis=0, keepdims=True))
        p = jnp.exp2(s - pltpu.repeat(m_next, width // 8, axis=0))
        alpha = jnp.exp2(m_prev - m_next)
        l_ref[...] = alpha * l_ref[...] + jnp.sum(p, axis=0, keepdims=True)
        vt = jnp.concatenate([vt_ref[0, 0, c0 + c] for c in range(width // 128)], axis=1)
        pv = jnp.dot(vt, p.astype(BF16), preferred_element_type=F32)
        acc_ref[...] = pltpu.repeat(alpha, V_DIM // 8, axis=0) * acc_ref[...] + pv
        m_ref[...] = m_next

    def start(j):
        return pl.multiple_of(j * tks, tks)

    cpt = tks // 128
    m_ref[...] = jnp.full_like(m_ref, -jnp.inf)
    l_ref[...] = jnp.zeros_like(l_ref)
    acc_ref[...] = jnp.zeros_like(acc_ref)
    sa_ref[...] = scores(0, tks)

    def pair(i, _):
        sb_ref[...] = scores(start(2 * i + 1), tks)
        update(sa_ref[...], 2 * i * cpt, tks)
        sa_ref[...] = scores(start(2 * i + 2), tks)
        update(sb_ref[...], (2 * i + 1) * cpt, tks)
        return 0

    n_pairs = n_sub // 2
    lax.fori_loop(0, n_pairs - 1, pair, 0, unroll=5 if (n_pairs - 1) % 5 == 0 else 1)
    done = 2 * (n_pairs - 1)
    sb_ref[...] = scores(done * tks + tks, tks)
    update(sa_ref[...], done * cpt, tks)
    tail = [(j * tks, tks) for j in range(done + 2, n_sub)] + ([(n_sub * tks, rem)] if rem else [])
    bufs = [sa_ref, sb_ref]
    pending = (sb_ref, (done + 1) * tks, tks)
    for idx, (r0, width) in enumerate(tail):
        nxt = bufs[idx % 2]
        nxt[:width, :] = scores(r0, width)
        update(pending[0][:pending[2], :], pending[1] // 128, pending[2])
        pending = (nxt, r0, width)
    update(pending[0][:pending[2], :], pending[1] // 128, pending[2])
    out_t = acc_ref[...] / pltpu.repeat(l_ref[...], V_DIM // 8, axis=0)
    o_ref[0] = out_t.T.astype(o_ref.dtype)


def _attention(qt, k, vt):
    b, h, _, l = qt.shape
    lk = k.shape[2]
    tq = _pick(l, (1024, 512, 256, 128))
    tks = 512
    n_sub, rem = lk // tks, lk % tks
    assert n_sub >= 2 and rem % 128 == 0
    return pl.pallas_call(
        functools.partial(_attn_kernel, tks=tks, n_sub=n_sub, rem=rem),
        grid=(b, h, l // tq),
        in_specs=[pl.BlockSpec((1, 1, QK_PAD, tq), lambda bi, hi, i: (bi, hi, 0, i)),
                  pl.BlockSpec((1, 1, lk, QK_PAD), lambda bi, hi, i: (bi, hi, 0, 0)),
                  pl.BlockSpec((1, 1, lk // 128, V_DIM, 128), lambda bi, hi, i: (bi, hi, 0, 0, 0))],
        out_specs=pl.BlockSpec((1, tq, V_DIM), lambda bi, hi, i: (bi, i, hi)),
        out_shape=jax.ShapeDtypeStruct((b, l, h * V_DIM), BF16),
        scratch_shapes=[pltpu.VMEM((8, tq), F32), pltpu.VMEM((8, tq), F32), pltpu.VMEM((V_DIM, tq), F32),
                        pltpu.VMEM((tks, tq), F32), pltpu.VMEM((tks, tq), F32)],
        name="flash_attention",
        compiler_params=_params(("arbitrary", "arbitrary", "arbitrary")),
    )(qt, k, vt)


def _conv_lru_layer(xc, xl, ml, mc, ng, w1, w2, layer, w_in, w_out, cv_w, cv_b, cv_g, cv_beta,
                    rg_conv_w, rg_conv_b, rg_wa, rg_ba, rg_wi, rg_bi, rg_lambda, need_ctx):
    w_in = w_in.astype(BF16)
    w_out = w_out.astype(BF16)
    pc = _norm_matmul(xc, mc, ng[0], w_in, 0, 1)
    plat = _norm_matmul(xl, ml, ng[0], w_in, 0, 1)
    c = cv_w.shape[1]
    zero = jnp.zeros((xl.shape[0], 8, c), F32)
    hs_c, hs_l = [], []
    for d, reverse in enumerate((False, True)):
        args = (rg_conv_w[d], rg_conv_b[d], rg_wa[d], rg_ba[d], rg_wi[d], rg_bi[d], rg_lambda[d], reverse)
        h_c, last_c = _rglru(pc, 2, zero, *args)
        h_l, _ = _rglru(plat, 2, last_c, *args)
        hs_c.append(h_c)
        hs_l.append(h_l)

    def finish(p, hs, x, mod):
        conv_out = _conformer_conv(p, cv_w, cv_b, cv_g, cv_beta)
        x = _mixout_ab(conv_out, hs[0], hs[1], p, x, mod, ng[1], w_out)
        return _mlp(x, mod, ng[2], ng[3], w1, w2, layer)

    xl = finish(plat, hs_l, xl, ml)
    if need_ctx:
        xc = finish(pc, hs_c, xc, mc)
    return xc, xl


def _fourier_mla_layer(xc, xl, ml, mc, ng, w1, w2, layer, w_in, w_out, q_g, kv_g, w_uq, w_ukv):
    b, l, d = xl.shape
    n_ctx = xc.shape[1]
    c = N_GROUPS * GROUP
    q_rank = q_g.shape[0]
    kv_rank = kv_g.shape[0]
    kr = w_in[:, c + q_rank + kv_rank:]
    kr1, kr2 = kr[:, :QK_ROPE // 2], kr[:, QK_ROPE // 2:]
    n_used = c + q_rank + kv_rank + 2 * QK_ROPE
    n_pad = -n_used % 256
    w_in_p = jnp.concatenate([w_in[:, :c + q_rank + kv_rank], kr1, kr2, kr2, kr1,
                              jnp.zeros((d, n_pad), w_in.dtype)], axis=1).astype(BF16)
    lk = l + n_ctx
    p = _norm_matmul(xl, ml, ng[0], w_in_p, 0, 1, out_rows=lk)
    tmc = _pick(n_ctx, (1024, 512, 256, 128))
    p = _norm_matmul(xc, mc, ng[0], w_in_p, 0, 1, out_rows=lk, row_block_off=l // tmc, into=p)
    cs = _rope_tables(l, n_ctx)
    four = _fourier_mix(p, l)
    q = _mla_queries(p, l, c // q_rank, q_g, w_uq, cs)
    k, v = _mla_keys_values(p, (c + q_rank) // kv_rank, (c + q_rank + kv_rank) // 128, kv_g, w_ukv, cs)
    att = _attention(q, k, v)
    xl = _mixout_cd(four, att, xl, ml, ng[1], w_out.astype(BF16))
    return _mlp(xl, ml, ng[2], ng[3], w1, w2, layer)


def kernel(x, c, ctx, c_ctx, mod_w, mod_b, norm_g, mlp_w1, mlp_w2, ab_w_in, ab_w_out, cv_w, cv_b, cv_norm_g, cv_norm_b, rg_conv_w, rg_conv_b, rg_wa, rg_ba, rg_wi, rg_bi, rg_lambda, cd_w_in, cd_w_out, mla_q_norm_g, mla_kv_norm_g, mla_w_uq, mla_w_ukv):
    b, l, d = x.shape
    depth = mod_w.shape[0]
    assert b + 1 <= 8 and l % GRID_W == 0
    cond8 = jnp.concatenate([c, c_ctx[None, :], jnp.zeros((8 - b - 1, d), F32)], axis=0)
    mods = _modulation(cond8, mod_w, mod_b)
    w1, w2 = mlp_w1.astype(BF16), mlp_w2.astype(BF16)
    xc, xl = ctx, x
    for i in range(depth):
        need_ctx = i < depth - 1
        ml = mods[i, :b].reshape(b, 6, d)
        mc = jnp.broadcast_to(mods[i, b].reshape(1, 6, d), (b, 6, d))
        j = i // 2
        if i % 2 == 0:
            xc, xl = _conv_lru_layer(xc, xl, ml, mc, norm_g[i], w1, w2, i, ab_w_in[j], ab_w_out[j],
                                     cv_w[j], cv_b[j], cv_norm_g[j], cv_norm_b[j], rg_conv_w[j], rg_conv_b[j],
                                     rg_wa[j], rg_ba[j], rg_wi[j], rg_bi[j], rg_lambda[j], need_ctx)
        else:
            xl = _fourier_mla_layer(xc, xl, ml, mc, norm_g[i], w1, w2, i, cd_w_in[j], cd_w_out[j],
                                    mla_q_norm_g[j], mla_kv_norm_g[j], mla_w_uq[j], mla_w_ukv[j])
    return xl
```

```python
import functools
import math

import numpy as np
import jax
import jax.numpy as jnp
from jax import lax
from jax.experimental import pallas as pl
from jax.experimental.pallas import tpu as pltpu

F32 = jnp.float32
BF16 = jnp.bfloat16

EPS = 1e-6
GRID_W = 64
N_GROUPS = 8
GROUP = 128
A_CONV = 31
RG_CONV = 4
RG_C = 8.0
QK_NOPE = 128
QK_ROPE = 64
QK_DIM = QK_NOPE + QK_ROPE
V_DIM = 128
ROPE_BASE = 10000.0
QK_PAD = 256
CONV_HALO = 16
LRU_HALO = 8
FFT_GROUP = 8
VMEM_LIMIT = 56 * 1024 * 1024


def _params(sem, vmem=VMEM_LIMIT):
    return pltpu.CompilerParams(dimension_semantics=sem, vmem_limit_bytes=vmem)


def _pick(n, candidates):
    for c in candidates:
        if n % c == 0:
            return c
    raise ValueError(f"no tile for {n} in {candidates}")


def _rms(x, g):
    return x * lax.rsqrt(jnp.mean(x * x, axis=-1, keepdims=True) + EPS) * g


def _mod_kernel(c_ref, w_ref, b_ref, o_ref):
    c = c_ref[...]
    s = c * jax.nn.sigmoid(c)
    o_ref[0] = jnp.dot(s, w_ref[0], preferred_element_type=F32,
                       precision=lax.Precision.HIGHEST) + b_ref[0]


def _modulation(cond8, mod_w, mod_b):
    depth, d, n = mod_w.shape
    tn = _pick(n, (1024, 512, 256, 128))
    return pl.pallas_call(
        _mod_kernel,
        grid=(depth, n // tn),
        in_specs=[pl.BlockSpec((8, d), lambda l, j: (0, 0)),
                  pl.BlockSpec((1, d, tn), lambda l, j: (l, 0, j)),
                  pl.BlockSpec((1, 1, tn), lambda l, j: (l, 0, j))],
        out_specs=pl.BlockSpec((1, 8, tn), lambda l, j: (l, 0, j)),
        out_shape=jax.ShapeDtypeStruct((depth, 8, n), F32),
        name="modulation",
        compiler_params=_params(("arbitrary", "arbitrary")),
    )(cond8, mod_w, mod_b.reshape(depth, 1, n))


def _norm_matmul_kernel(x_ref, m_ref, g_ref, w_ref, *rest, shift_row, scale_row):
    o_ref = rest[-1]
    gain = g_ref[...] * (1.0 + m_ref[0, scale_row:scale_row + 1, :])
    x = x_ref[0]
    h = x * lax.rsqrt(jnp.mean(x * x, axis=-1, keepdims=True) + EPS) * gain + m_ref[0, shift_row:shift_row + 1, :]
    o_ref[0] = jnp.dot(h.astype(BF16), w_ref[...], preferred_element_type=F32).astype(o_ref.dtype)


def _norm_matmul(x, mod, g, w, shift_row, scale_row, out_rows=None, row_block_off=0, into=None):
    b, l, d = x.shape
    n = w.shape[1]
    tm = _pick(l, (512, 256, 128))
    out_rows = l if out_rows is None else out_rows
    off = row_block_off
    in_specs = [pl.BlockSpec((1, tm, d), lambda bi, i: (bi, i, 0)),
                pl.BlockSpec((1, 6, d), lambda bi, i: (bi, 0, 0)),
                pl.BlockSpec((1, d), lambda bi, i: (0, 0)),
                pl.BlockSpec((d, n), lambda bi, i: (0, 0), pipeline_mode=pl.Buffered(1))]
    args = [x, mod, g.reshape(1, d), w]
    aliases = {}
    if into is not None:
        in_specs.append(pl.BlockSpec(memory_space=pl.ANY))
        args.append(into)
        aliases = {4: 0}
    return pl.pallas_call(
        functools.partial(_norm_matmul_kernel, shift_row=shift_row, scale_row=scale_row),
        grid=(b, l // tm),
        in_specs=in_specs,
        out_specs=pl.BlockSpec((1, tm, n), lambda bi, i: (bi, i + off, 0)),
        out_shape=jax.ShapeDtypeStruct((b, out_rows, n), F32),
        input_output_aliases=aliases,
        name="norm_matmul",
        compiler_params=_params(("arbitrary", "arbitrary")),
    )(*args)


def _conv_kernel(v_ref, gt_ref, vp_ref, gp_ref, vn_ref, gn_ref, w_ref, b_ref, lg_ref, lb_ref,
                 o_ref, ubuf, sh_ref, *, t, rc):
    i = pl.program_id(1)
    first = i == 0
    last = i == pl.num_programs(1) - 1
    halo = CONV_HALO
    ubuf[halo:halo + t, :] = v_ref[0] * jax.nn.sigmoid(gt_ref[0])
    up = vp_ref[0] * jax.nn.sigmoid(gp_ref[0])
    un = vn_ref[0] * jax.nn.sigmoid(gn_ref[0])
    ubuf[0:halo, :] = jnp.where(first, 0.0, up)
    ubuf[halo + t:2 * halo + t, :] = jnp.where(last, 0.0, un)
    base = halo - A_CONV // 2
    span = t + 8 * ((base + A_CONV - 1) // 8)
    for s in range(1, 8):
        sh_ref[s - 1] = ubuf[s:s + span, :]
    for r0 in range(0, t, rc):
        for g in range(N_GROUPS):
            cs = slice(g * GROUP, (g + 1) * GROUP)
            acc = jnp.zeros((rc, GROUP), F32)
            for k in range(A_CONV):
                s, a8 = (base + k) % 8, 8 * ((base + k) // 8)
                src = ubuf if s == 0 else sh_ref.at[s - 1]
                acc = acc + w_ref[k:k + 1, cs] * src[r0 + a8:r0 + a8 + rc, cs]
            y = acc + b_ref[:, cs]
            mu = jnp.mean(y, axis=-1, keepdims=True)
            dlt = y - mu
            yn = dlt * lax.rsqrt(jnp.mean(dlt * dlt, axis=-1, keepdims=True) + EPS)
            z = yn * lg_ref[:, cs] + lb_ref[:, cs]
            o_ref[0, r0:r0 + rc, cs] = (z * jax.nn.sigmoid(z)).astype(o_ref.dtype)


def _conformer_conv(p, cv_w, cv_b, ln_g, ln_b):
    b, l, _ = p.shape
    c = cv_w.shape[1]
    t = _pick(l, (256, 128))
    rc = 64
    hb = t // CONV_HALO
    nhb = l // CONV_HALO
    main = lambda col: pl.BlockSpec((1, t, c), lambda bi, i: (bi, i, col))
    prev = lambda col: pl.BlockSpec((1, CONV_HALO, c), lambda bi, i: (bi, jnp.maximum(i * hb - 1, 0), col))
    nxt = lambda col: pl.BlockSpec((1, CONV_HALO, c), lambda bi, i: (bi, jnp.minimum((i + 1) * hb, nhb - 1), col))
    vec = pl.BlockSpec((1, c), lambda bi, i: (0, 0))
    return pl.pallas_call(
        functools.partial(_conv_kernel, t=t, rc=rc),
        grid=(b, l // t),
        in_specs=[main(0), main(1), prev(0), prev(1), nxt(0), nxt(1),
                  pl.BlockSpec((A_CONV, c), lambda bi, i: (0, 0)), vec, vec, vec],
        out_specs=pl.BlockSpec((1, t, c), lambda bi, i: (bi, i, 0)),
        out_shape=jax.ShapeDtypeStruct((b, l, c), BF16),
        scratch_shapes=[pltpu.VMEM((t + 2 * CONV_HALO, c), F32),
                        pltpu.VMEM((7, t + 8 * ((CONV_HALO + A_CONV // 2) // 8), c), F32)],
        name="conformer_conv",
        compiler_params=_params(("arbitrary", "arbitrary")),
    )(p, p, p, p, p, p, cv_w, cv_b.reshape(1, c), ln_g.reshape(1, c), ln_b.reshape(1, c))


def _lru_kernel(x_ref, xh_ref, cw_ref, cb_ref, wg_ref, ba_ref, bi_ref, lam_ref, h0_ref,
                h_ref, hl_ref, xbuf, a_buf, b_buf, carry, *, t, reverse):
    i = pl.program_id(1)
    c = x_ref.shape[-1]
    halo = LRU_HALO

    @pl.when(i == 0)
    def _():
        carry[...] = h0_ref[0]

    edge = jnp.where(i == 0, 0.0, xh_ref[0])
    if reverse:
        xbuf[0:t, :] = x_ref[0]
        xbuf[t:t + halo, :] = edge
        base = 0
    else:
        xbuf[0:halo, :] = edge
        xbuf[halo:halo + t, :] = x_ref[0]
        base = halo - (RG_CONV - 1)
    xall = xbuf[...]
    n = t + halo
    y = jnp.zeros((t, c), F32) + cb_ref[...]
    for k in range(RG_CONV):
        win = xall if base + k == 0 else pltpu.roll(xall, n - (base + k), 0)
        y = y + cw_ref[k:k + 1, :] * win[:t]

    log_sig = jax.nn.log_sigmoid(lam_ref[...])
    for h in range(N_GROUPS):
        cs = slice(h * GROUP, (h + 1) * GROUP)
        yh = y[:, cs]
        gates = jnp.dot(yh.astype(BF16), wg_ref[h], preferred_element_type=F32)
        r = jax.nn.sigmoid(gates[:, :GROUP] + ba_ref[:, cs])
        ig = jax.nn.sigmoid(gates[:, GROUP:] + bi_ref[:, cs])
        log_a = RG_C * r * log_sig[:, cs]
        a = jnp.exp(log_a)
        a_buf[:, cs] = a
        b_buf[:, cs] = jnp.sqrt(1.0 - a * a) * (ig * yh)

    ngroups = t // 8
    row = lax.broadcasted_iota(jnp.int32, (8, c), 0)

    def body(gidx, cr):
        gi = (ngroups - 1 - gidx) if reverse else gidx
        r0 = pl.multiple_of(gi * 8, 8)
        a = a_buf[pl.ds(r0, 8), :]
        bb = b_buf[pl.ds(r0, 8), :]
        for k in (1, 2, 4):
            if reverse:
                a_s = pltpu.roll(a, 8 - k, 0)
                b_s = pltpu.roll(bb, 8 - k, 0)
                valid = row < 8 - k
            else:
                a_s = pltpu.roll(a, k, 0)
                b_s = pltpu.roll(bb, k, 0)
                valid = row >= k
            bb = jnp.where(valid, a * b_s + bb, bb)
            a = jnp.where(valid, a * a_s, a)
        hh = bb + a * cr
        h_ref[0, pl.ds(r0, 8), :] = hh
        edge_row = hh[0:1, :] if reverse else hh[7:8, :]
        return jnp.broadcast_to(edge_row, (8, c))

    cr = lax.fori_loop(0, ngroups, body, carry[...], unroll=4)
    carry[...] = cr
    hl_ref[0] = cr


def _rglru(p, col, h0, conv_w, conv_b, wa, ba, wi, bi, lam, reverse):
    b, l, _ = p.shape
    c = conv_w.shape[1]
    t = _pick(l, (256, 128))
    nt = l // t
    hb = t // LRU_HALO
    nhb = l // LRU_HALO
    if reverse:
        tile = lambda bi_, i: (bi_, nt - 1 - i, col)
        halo = lambda bi_, i: (bi_, jnp.minimum((nt - i) * hb, nhb - 1), col)
        otile = lambda bi_, i: (bi_, nt - 1 - i, 0)
    else:
        tile = lambda bi_, i: (bi_, i, col)
        halo = lambda bi_, i: (bi_, jnp.maximum(i * hb - 1, 0), col)
        otile = lambda bi_, i: (bi_, i, 0)
    wg = jnp.concatenate([wa, wi], axis=-1).astype(BF16)
    vec = pl.BlockSpec((1, c), lambda bi_, i: (0, 0))
    state = pl.BlockSpec((1, 8, c), lambda bi_, i: (bi_, 0, 0))
    return pl.pallas_call(
        functools.partial(_lru_kernel, t=t, reverse=reverse),
        grid=(b, nt),
        in_specs=[pl.BlockSpec((1, t, c), tile),
                  pl.BlockSpec((1, LRU_HALO, c), halo),
                  pl.BlockSpec((RG_CONV, c), lambda bi_, i: (0, 0)), vec,
                  pl.BlockSpec((N_GROUPS, GROUP, 2 * GROUP), lambda bi_, i: (0, 0, 0)),
                  vec, vec, vec, state],
        out_specs=[pl.BlockSpec((1, t, c), otile), state],
        out_shape=[jax.ShapeDtypeStruct((b, l, c), F32), jax.ShapeDtypeStruct((b, 8, c), F32)],
        scratch_shapes=[pltpu.VMEM((t + LRU_HALO, c), F32), pltpu.VMEM((t, c), F32),
                        pltpu.VMEM((t, c), F32), pltpu.VMEM((8, c), F32)],
        name="rglru_rev" if reverse else "rglru_fwd",
        compiler_params=_params(("arbitrary", "arbitrary")),
    )(p, p, conv_w, conv_b.reshape(1, c), wg, ba.reshape(1, c), bi.reshape(1, c), lam.reshape(1, c), h0)


def _mixout_ab_kernel(cv_ref, hf_ref, hr_ref, pg_ref, x_ref, m_ref, g_ref, wa_ref, wb_ref, o_ref):
    rec = (hf_ref[0] + hr_ref[0]) * jax.nn.gelu(pg_ref[0])
    y = jnp.dot(cv_ref[0], wa_ref[...], preferred_element_type=F32)
    y = y + jnp.dot(rec.astype(BF16), wb_ref[...], preferred_element_type=F32)
    o_ref[0] = x_ref[0] + m_ref[0, 2:3, :] * _rms(y, g_ref[...])


def _mixout_ab(conv_out, h_f, h_r, p, x, mod, g, w_out):
    b, l, d = x.shape
    c = conv_out.shape[-1]
    tm = _pick(l, (512, 256, 128))
    half = lambda: pl.BlockSpec((1, tm, c), lambda bi, i: (bi, i, 0))
    wspec = lambda k: pl.BlockSpec((c, d), lambda bi, i: (k, 0))
    return pl.pallas_call(
        _mixout_ab_kernel,
        grid=(b, l // tm),
        in_specs=[half(), half(), half(),
                  pl.BlockSpec((1, tm, c), lambda bi, i: (bi, i, 3)),
                  pl.BlockSpec((1, tm, d), lambda bi, i: (bi, i, 0)),
                  pl.BlockSpec((1, 6, d), lambda bi, i: (bi, 0, 0)),
                  pl.BlockSpec((1, d), lambda bi, i: (0, 0)),
                  wspec(0), wspec(1)],
        out_specs=pl.BlockSpec((1, tm, d), lambda bi, i: (bi, i, 0)),
        out_shape=jax.ShapeDtypeStruct((b, l, d), F32),
        name="mixout_ab",
        compiler_params=_params(("arbitrary", "arbitrary")),
    )(conv_out, h_f, h_r, p, x, mod, g.reshape(1, d), w_out, w_out)


def _mixout_cd_kernel(f_ref, o_att_ref, x_ref, m_ref, g_ref, wa_ref, wb_ref, o_ref):
    y = jnp.dot(f_ref[0].astype(BF16), wa_ref[...], preferred_element_type=F32)
    y = y + jnp.dot(o_att_ref[0], wb_ref[...], preferred_element_type=F32)
    o_ref[0] = x_ref[0] + m_ref[0, 2:3, :] * _rms(y, g_ref[...])


def _mixout_cd(four, att, x, mod, g, w_out):
    b, l, d = x.shape
    c = four.shape[-1]
    tm = _pick(l, (512, 256, 128))
    half = lambda: pl.BlockSpec((1, tm, c), lambda bi, i: (bi, i, 0))
    wspec = lambda k: pl.BlockSpec((c, d), lambda bi, i: (k, 0))
    return pl.pallas_call(
        _mixout_cd_kernel,
        grid=(b, l // tm),
        in_specs=[half(), half(),
                  pl.BlockSpec((1, tm, d), lambda bi, i: (bi, i, 0)),
                  pl.BlockSpec((1, 6, d), lambda bi, i: (bi, 0, 0)),
                  pl.BlockSpec((1, d), lambda bi, i: (0, 0)),
                  wspec(0), wspec(1)],
        out_specs=pl.BlockSpec((1, tm, d), lambda bi, i: (bi, i, 0)),
        out_shape=jax.ShapeDtypeStruct((b, l, d), F32),
        name="mixout_cd",
        compiler_params=_params(("arbitrary", "arbitrary")),
    )(four, att, x, mod, g.reshape(1, d), w_out, w_out)


def _mlp_kernel(x_ref, m_ref, gpre_ref, gpost_ref, w1_ref, w2_ref, o_ref, h_ref, acc_ref):
    j = pl.program_id(2)
    last = pl.num_programs(2) - 1

    def chunk(h):
        u = jnp.dot(h, w1_ref[...], preferred_element_type=F32)
        u = jnp.square(jnp.maximum(u, 0.0))
        return jnp.dot(u.astype(BF16), w2_ref[...], preferred_element_type=F32)

    @pl.when(j == 0)
    def _():
        gain = gpre_ref[...] * (1.0 + m_ref[0, 4:5, :])
        x = x_ref[0]
        h = x * lax.rsqrt(jnp.mean(x * x, axis=-1, keepdims=True) + EPS) * gain + m_ref[0, 3:4, :]
        h = h.astype(BF16)
        h_ref[...] = h
        acc_ref[...] = chunk(h)

    @pl.when((j > 0) & (j < last))
    def _():
        acc_ref[...] += chunk(h_ref[...])

    @pl.when(j == last)
    def _():
        y = acc_ref[...] + chunk(h_ref[...])
        gain = m_ref[0, 5:6, :] * gpost_ref[...]
        o_ref[0] = x_ref[0] + y * lax.rsqrt(jnp.mean(y * y, axis=-1, keepdims=True) + EPS) * gain


def _mlp(x, mod, g_pre, g_post, w1, w2, layer):
    b, l, d = x.shape
    f = w1.shape[2]
    tm = _pick(l, (512, 256, 128))
    tf = _pick(f, (1024, 512, 256, 128))
    assert f // tf >= 2
    return pl.pallas_call(
        _mlp_kernel,
        grid=(b, l // tm, f // tf),
        in_specs=[pl.BlockSpec((1, tm, d), lambda bi, i, j: (bi, i, 0)),
                  pl.BlockSpec((1, 6, d), lambda bi, i, j: (bi, 0, 0)),
                  pl.BlockSpec((1, d), lambda bi, i, j: (0, 0)),
                  pl.BlockSpec((1, d), lambda bi, i, j: (0, 0)),
                  pl.BlockSpec((None, d, tf), lambda bi, i, j: (layer, 0, j)),
                  pl.BlockSpec((None, tf, d), lambda bi, i, j: (layer, j, 0))],
        out_specs=pl.BlockSpec((1, tm, d), lambda bi, i, j: (bi, i, 0)),
        out_shape=jax.ShapeDtypeStruct((b, l, d), F32),
        scratch_shapes=[pltpu.VMEM((tm, d), BF16), pltpu.VMEM((tm, d), F32)],
        name="mlp",
        compiler_params=_params(("arbitrary", "arbitrary", "arbitrary")),
    )(x, mod, g_pre.reshape(1, d), g_post.reshape(1, d), w1, w2)


def _fourier_tables(l):
    n1 = 1 << (int(math.log2(l)) // 2)
    while l % n1:
        n1 //= 2
    n2 = l // n1
    def cs(n):
        idx = np.arange(n)
        ang = 2.0 * np.pi * ((idx[:, None] * idx[None, :]) % n) / n
        return np.cos(ang), np.sin(ang)
    c1, s1 = cs(n1)
    c2, s2 = cs(n2)
    cc, sc = cs(GROUP)
    delta = 2.0 * np.pi * np.arange(n1)[:, None] / l * np.ones((1, GROUP))
    norm = 1.0 / math.sqrt(l * GROUP)
    chan = np.concatenate([cc, -sc], axis=1)
    f1 = np.concatenate([c1, s1], axis=0)
    f2 = np.concatenate([c2, s2], axis=1) * norm
    return (n1, n2, jnp.asarray(chan, BF16), jnp.asarray(f1, BF16), jnp.asarray(f2, BF16),
            jnp.asarray(np.cos(delta), F32), jnp.asarray(np.sin(delta), F32))


def _fourier_kernel(x_ref, chan_ref, f1_ref, f2_ref, dc_ref, ds_ref, o_ref, are_buf, aim_buf, tw_c, tw_s,
                    *, n1, n2):
    tw_c[...] = jnp.ones_like(tw_c)
    tw_s[...] = jnp.zeros_like(tw_s)

    def stage1(jg, _):
        j0 = jg * FFT_GROUP
        xs = jnp.concatenate([x_ref[0, pl.ds(j0 + u, n1, stride=n2), :] for u in range(FFT_GROUP)], axis=0)
        pq = jnp.dot(xs.astype(BF16), chan_ref[...], preferred_element_type=F32)
        pq = jnp.concatenate([pq[u * n1:(u + 1) * n1] for u in range(FFT_GROUP)], axis=1).astype(BF16)
        r = jnp.dot(f1_ref[...], pq, preferred_element_type=F32)
        for u in range(FFT_GROUP):
            ru = r[:, u * 2 * GROUP:(u + 1) * 2 * GROUP]
            a_re = ru[:n1, :GROUP] + ru[n1:, GROUP:]
            a_im = ru[:n1, GROUP:] - ru[n1:, :GROUP]
            tc = tw_c[...]
            ts = tw_s[...]
            row0 = pl.multiple_of((j0 + u) * n1, 8)
            are_buf[pl.ds(row0, n1), :] = a_re * tc + a_im * ts
            aim_buf[pl.ds(row0, n1), :] = a_im * tc - a_re * ts
            tw_c[...] = tc * dc_ref[...] - ts * ds_ref[...]
            tw_s[...] = ts * dc_ref[...] + tc * ds_ref[...]
        return 0

    lax.fori_loop(0, n2 // FFT_GROUP, stage1, 0)

    def stage2(kg, _):
        k0 = kg * FFT_GROUP
        rhs = jnp.concatenate(
            [jnp.concatenate([are_buf[pl.ds(k0 + u, n2, stride=n1), :], aim_buf[pl.ds(k0 + u, n2, stride=n1), :]],
                             axis=0) for u in range(FFT_GROUP)], axis=1).astype(BF16)
        y = jnp.dot(f2_ref[...], rhs, preferred_element_type=F32)
        for u in range(FFT_GROUP):
            o_ref[0, pl.ds(k0 + u, n2, stride=n1), :] = y[:, u * GROUP:(u + 1) * GROUP]
        return 0

    lax.fori_loop(0, n1 // FFT_GROUP, stage2, 0)


def _fourier_mix(p, l):
    b = p.shape[0]
    n1, n2, chan, f1, f2, dc, ds = _fourier_tables(l)
    const = lambda a: pl.BlockSpec(a.shape, lambda bi, g: (0, 0))
    return pl.pallas_call(
        functools.partial(_fourier_kernel, n1=n1, n2=n2),
        grid=(b, N_GROUPS),
        in_specs=[pl.BlockSpec((1, l, GROUP), lambda bi, g: (bi, 0, g)),
                  const(chan), const(f1), const(f2), const(dc), const(ds)],
        out_specs=pl.BlockSpec((1, l, GROUP), lambda bi, g: (bi, 0, g)),
        out_shape=jax.ShapeDtypeStruct((b, l, N_GROUPS * GROUP), F32),
        scratch_shapes=[pltpu.VMEM((l, GROUP), F32), pltpu.VMEM((l, GROUP), F32),
                        pltpu.VMEM((n1, GROUP), F32), pltpu.VMEM((n1, GROUP), F32)],
        name="fourier_mix",
        compiler_params=_params(("arbitrary", "arbitrary"), 62 * 1024 * 1024),
    )(p, chan, f1, f2, dc, ds)


def _rope_tables(l, n_ctx):
    t = np.arange(l)
    inv = ROPE_BASE ** (-np.arange(QK_ROPE // 4, dtype=np.float64) / (QK_ROPE // 4))
    ang = np.concatenate([(t // GRID_W)[:, None] * inv, (t % GRID_W)[:, None] * inv], axis=-1)
    ang = np.concatenate([ang, np.zeros((n_ctx, QK_ROPE // 2))], axis=0)
    cos, sin = np.cos(ang), np.sin(ang)
    return jnp.asarray(np.concatenate([cos, cos, -sin, sin], axis=-1), F32)


def _rope_apply(r, cs):
    tt = r * cs
    ro = tt + pltpu.roll(tt, QK_ROPE, 1)
    lane = lax.broadcasted_iota(jnp.int32, ro.shape, 1)
    return jnp.where(lane < QK_ROPE, ro, 0.0)


def _q_kernel(cq_ref, g_ref, w_ref, cs_ref, o_ref):
    n = _rms(cq_ref[0], g_ref[...]).astype(BF16)
    cs = cs_ref[...]
    for h in range(N_GROUPS):
        res = jnp.dot(n, w_ref[h], preferred_element_type=F32)
        q = jnp.concatenate([res[:, :QK_NOPE], _rope_apply(res[:, QK_NOPE:], cs)], axis=1)
        q = q * (math.log2(math.e) * QK_DIM ** -0.5)
        o_ref[0, h] = q.T.astype(o_ref.dtype)


def _mla_queries(p, l, col, q_g, w_uq, cs):
    b = p.shape[0]
    rank = w_uq.shape[0]
    w = w_uq.reshape(rank, N_GROUPS, QK_DIM)
    x1 = w[:, :, QK_NOPE:QK_NOPE + QK_ROPE // 2]
    x2 = w[:, :, QK_NOPE + QK_ROPE // 2:]
    w = jnp.concatenate([w[:, :, :QK_NOPE], x1, x2, x2, x1], axis=-1)
    w = jnp.transpose(w, (1, 0, 2)).astype(BF16)
    tm = _pick(l, (1024, 512, 256, 128))
    return pl.pallas_call(
        _q_kernel,
        grid=(b, l // tm),
        in_specs=[pl.BlockSpec((1, tm, rank), lambda bi, i: (bi, i, col)),
                  pl.BlockSpec((1, rank), lambda bi, i: (0, 0)),
                  pl.BlockSpec((N_GROUPS, rank, QK_PAD), lambda bi, i: (0, 0, 0)),
                  pl.BlockSpec((tm, 128), lambda bi, i: (i, 0))],
        out_specs=pl.BlockSpec((1, N_GROUPS, QK_PAD, tm), lambda bi, i: (bi, 0, 0, i)),
        out_shape=jax.ShapeDtypeStruct((b, N_GROUPS, QK_PAD, l), BF16),
        name="mla_q",
        compiler_params=_params(("arbitrary", "arbitrary")),
    )(p, q_g.reshape(1, rank), w, cs)


def _kv_kernel(ckv_ref, kr_ref, g_ref, w_ref, cs_ref, k_ref, v_ref):
    n = _rms(ckv_ref[0], g_ref[...]).astype(BF16)
    rope = _rope_apply(kr_ref[0], cs_ref[...]).astype(BF16)
    hw = QK_NOPE + V_DIM
    for h in range(N_GROUPS):
        res = jnp.dot(n, w_ref[:, h * hw:(h + 1) * hw], preferred_element_type=F32)
        k_ref[0, h] = jnp.concatenate([res[:, :QK_NOPE].astype(BF16), rope], axis=1)
        vt = res[:, QK_NOPE:].T.astype(BF16)
        for c in range(vt.shape[1] // 128):
            v_ref[0, h, c] = vt[:, c * 128:(c + 1) * 128]


def _mla_keys_values(p, col_kv, col_kr, kv_g, w_ukv, cs):
    b, lk, _ = p.shape
    rank = w_ukv.shape[0]
    tm = _pick(lk, (1280, 640, 256, 128))
    return pl.pallas_call(
        _kv_kernel,
        grid=(b, lk // tm),
        in_specs=[pl.BlockSpec((1, tm, rank), lambda bi, i: (bi, i, col_kv)),
                  pl.BlockSpec((1, tm, 128), lambda bi, i: (bi, i, col_kr)),
                  pl.BlockSpec((1, rank), lambda bi, i: (0, 0)),
                  pl.BlockSpec(w_ukv.shape, lambda bi, i: (0, 0)),
                  pl.BlockSpec((tm, 128), lambda bi, i: (i, 0))],
        out_specs=[pl.BlockSpec((1, N_GROUPS, tm, QK_PAD), lambda bi, i: (bi, 0, i, 0)),
                   pl.BlockSpec((1, N_GROUPS, tm // 128, V_DIM, 128), lambda bi, i: (bi, 0, i, 0, 0))],
        out_shape=[jax.ShapeDtypeStruct((b, N_GROUPS, lk, QK_PAD), BF16),
                   jax.ShapeDtypeStruct((b, N_GROUPS, lk // 128, V_DIM, 128), BF16)],
        name="mla_kv",
        compiler_params=_params(("arbitrary", "arbitrary")),
    )(p, p, kv_g.reshape(1, rank), w_ukv.astype(BF16), cs)


def _attn_kernel(qt_ref, k_ref, vt_ref, o_ref, m_ref, l_ref, acc_ref, sa_ref, sb_ref, *, tks, n_sub, rem):
    qt = qt_ref[0, 0]

    def scores(r0, width):
        return jnp.dot(k_ref[0, 0, pl.ds(r0, width), :], qt, preferred_element_type=F32)

    def update(s, c0, width):
        m_prev = m_ref[...]
        m_next = jnp.maximum(m_prev, jnp.max(s, axis=0, keepdims=True))
        p = jnp.exp2(s - pltpu.repeat(m_next, width // 8, axis=0))
        alpha = jnp.exp2(m_prev - m_next)
        l_ref[...] = alpha * l_ref[...] + jnp.sum(p, axis=0, keepdims=True)
        vt = jnp.concatenate([vt_ref[0, 0, c0 + c] for c in range(width // 128)], axis=1)
        pv = jnp.dot(vt, p.astype(BF16), preferred_element_type=F32)
        acc_ref[...] = pltpu.repeat(alpha, V_DIM // 8, axis=0) * acc_ref[...] + pv
        m_ref[...] = m_next

    def start(j):
        return pl.multiple_of(j * tks, tks)

    cpt = tks // 128
    m_ref[...] = jnp.full_like(m_ref, -jnp.inf)
    l_ref[...] = jnp.zeros_like(l_ref)
    acc_ref[...] = jnp.zeros_like(acc_ref)
    sa_ref[...] = scores(0, tks)

    def pair(i, _):
        sb_ref[...] = scores(start(2 * i + 1), tks)
        update(sa_ref[...], 2 * i * cpt, tks)
        sa_ref[...] = scores(start(2 * i + 2), tks)
        update(sb_ref[...], (2 * i + 1) * cpt, tks)
        return 0

    n_pairs = n_sub // 2
    lax.fori_loop(0, n_pairs - 1, pair, 0, unroll=5 if (n_pairs - 1) % 5 == 0 else 1)
    done = 2 * (n_pairs - 1)
    sb_ref[...] = scores(done * tks + tks, tks)
    update(sa_ref[...], done * cpt, tks)
    tail = [(j * tks, tks) for j in range(done + 2, n_sub)] + ([(n_sub * tks, rem)] if rem else [])
    bufs = [sa_ref, sb_ref]
    pending = (sb_ref, (done + 1) * tks, tks)
    for idx, (r0, width) in enumerate(tail):
        nxt = bufs[idx % 2]
        nxt[:width, :] = scores(r0, width)
        update(pending[0][:pending[2], :], pending[1] // 128, pending[2])
        pending = (nxt, r0, width)
    update(pending[0][:pending[2], :], pending[1] // 128, pending[2])
    out_t = acc_ref[...] / pltpu.repeat(l_ref[...], V_DIM // 8, axis=0)
    o_ref[0] = out_t.T.astype(o_ref.dtype)


def _attention(qt, k, vt):
    b, h, _, l = qt.shape
    lk = k.shape[2]
    tq = _pick(l, (1024, 512, 256, 128))
    tks = 512
    n_sub, rem = lk // tks, lk % tks
    assert n_sub >= 2 and rem % 128 == 0
    return pl.pallas_call(
        functools.partial(_attn_kernel, tks=tks, n_sub=n_sub, rem=rem),
        grid=(b, h, l // tq),
        in_specs=[pl.BlockSpec((1, 1, QK_PAD, tq), lambda bi, hi, i: (bi, hi, 0, i)),
                  pl.BlockSpec((1, 1, lk, QK_PAD), lambda bi, hi, i: (bi, hi, 0, 0)),
                  pl.BlockSpec((1, 1, lk // 128, V_DIM, 128), lambda bi, hi, i: (bi, hi, 0, 0, 0))],
        out_specs=pl.BlockSpec((1, tq, V_DIM), lambda bi, hi, i: (bi, i, hi)),
        out_shape=jax.ShapeDtypeStruct((b, l, h * V_DIM), BF16),
        scratch_shapes=[pltpu.VMEM((8, tq), F32), pltpu.VMEM((8, tq), F32), pltpu.VMEM((V_DIM, tq), F32),
                        pltpu.VMEM((tks, tq), F32), pltpu.VMEM((tks, tq), F32)],
        name="flash_attention",
        compiler_params=_params(("arbitrary", "arbitrary", "arbitrary")),
    )(qt, k, vt)


def _conv_lru_layer(xc, xl, ml, mc, ng, w1, w2, layer, w_in, w_out, cv_w, cv_b, cv_g, cv_beta,
                    rg_conv_w, rg_conv_b, rg_wa, rg_ba, rg_wi, rg_bi, rg_lambda, need_ctx):
    w_in = w_in.astype(BF16)
    w_out = w_out.astype(BF16)
    pc = _norm_matmul(xc, mc, ng[0], w_in, 0, 1)
    plat = _norm_matmul(xl, ml, ng[0], w_in, 0, 1)
    c = cv_w.shape[1]
    zero = jnp.zeros((xl.shape[0], 8, c), F32)
    hs_c, hs_l = [], []
    for d, reverse in enumerate((False, True)):
        args = (rg_conv_w[d], rg_conv_b[d], rg_wa[d], rg_ba[d], rg_wi[d], rg_bi[d], rg_lambda[d], reverse)
        h_c, last_c = _rglru(pc, 2, zero, *args)
        h_l, _ = _rglru(plat, 2, last_c, *args)
        hs_c.append(h_c)
        hs_l.append(h_l)

    def finish(p, hs, x, mod):
        conv_out = _conformer_conv(p, cv_w, cv_b, cv_g, cv_beta)
        x = _mixout_ab(conv_out, hs[0], hs[1], p, x, mod, ng[1], w_out)
        return _mlp(x, mod, ng[2], ng[3], w1, w2, layer)

    xl = finish(plat, hs_l, xl, ml)
    if need_ctx:
        xc = finish(pc, hs_c, xc, mc)
    return xc, xl


def _fourier_mla_layer(xc, xl, ml, mc, ng, w1, w2, layer, w_in, w_out, q_g, kv_g, w_uq, w_ukv):
    b, l, d = xl.shape
    n_ctx = xc.shape[1]
    c = N_GROUPS * GROUP
    q_rank = q_g.shape[0]
    kv_rank = kv_g.shape[0]
    kr = w_in[:, c + q_rank + kv_rank:]
    kr1, kr2 = kr[:, :QK_ROPE // 2], kr[:, QK_ROPE // 2:]
    n_used = c + q_rank + kv_rank + 2 * QK_ROPE
    n_pad = -n_used % 256
    w_in_p = jnp.concatenate([w_in[:, :c + q_rank + kv_rank], kr1, kr2, kr2, kr1,
                              jnp.zeros((d, n_pad), w_in.dtype)], axis=1).astype(BF16)
    lk = l + n_ctx
    p = _norm_matmul(xl, ml, ng[0], w_in_p, 0, 1, out_rows=lk)
    tmc = _pick(n_ctx, (512, 256, 128))
    p = _norm_matmul(xc, mc, ng[0], w_in_p, 0, 1, out_rows=lk, row_block_off=l // tmc, into=p)
    cs = _rope_tables(l, n_ctx)
    four = _fourier_mix(p, l)
    q = _mla_queries(p, l, c // q_rank, q_g, w_uq, cs)
    k, v = _mla_keys_values(p, (c + q_rank) // kv_rank, (c + q_rank + kv_rank) // 128, kv_g, w_ukv, cs)
    att = _attention(q, k, v)
    xl = _mixout_cd(four, att, xl, ml, ng[1], w_out.astype(BF16))
    return _mlp(xl, ml, ng[2], ng[3], w1, w2, layer)


def kernel(x, c, ctx, c_ctx, mod_w, mod_b, norm_g, mlp_w1, mlp_w2, ab_w_in, ab_w_out, cv_w, cv_b, cv_norm_g, cv_norm_b, rg_conv_w, rg_conv_b, rg_wa, rg_ba, rg_wi, rg_bi, rg_lambda, cd_w_in, cd_w_out, mla_q_norm_g, mla_kv_norm_g, mla_w_uq, mla_w_ukv):
    b, l, d = x.shape
    depth = mod_w.shape[0]
    assert b + 1 <= 8 and l % GRID_W == 0
    cond8 = jnp.concatenate([c, c_ctx[None, :], jnp.zeros((8 - b - 1, d), F32)], axis=0)
    mods = _modulation(cond8, mod_w, mod_b)
    w1, w2 = mlp_w1.astype(BF16), mlp_w2.astype(BF16)
    xc, xl = ctx, x
    for i in range(depth):
        need_ctx = i < depth - 1
        ml = mods[i, :b].reshape(b, 6, d)
        mc = jnp.broadcast_to(mods[i, b].reshape(1, 6, d), (b, 6, d))
        j = i // 2
        if i % 2 == 0:
            xc, xl = _conv_lru_layer(xc, xl, ml, mc, norm_g[i], w1, w2, i, ab_w_in[j], ab_w_out[j],
                                     cv_w[j], cv_b[j], cv_norm_g[j], cv_norm_b[j], rg_conv_w[j], rg_conv_b[j],
                                     rg_wa[j], rg_ba[j], rg_wi[j], rg_bi[j], rg_lambda[j], need_ctx)
        else:
            xl = _fourier_mla_layer(xc, xl, ml, mc, norm_g[i], w1, w2, i, cd_w_in[j], cd_w_out[j],
                                    mla_q_norm_g[j], mla_kv_norm_g[j], mla_w_uq[j], mla_w_ukv[j])
    return xl
```

```python
import functools
import math

import numpy as np
import jax
import jax.numpy as jnp
from jax import lax
from jax.experimental import pallas as pl
from jax.experimental.pallas import tpu as pltpu

F32 = jnp.float32
BF16 = jnp.bfloat16

EPS = 1e-6
GRID_W = 64
N_GROUPS = 8
GROUP = 128
A_CONV = 31
RG_CONV = 4
RG_C = 8.0
QK_NOPE = 128
QK_ROPE = 64
QK_DIM = QK_NOPE + QK_ROPE
V_DIM = 128
ROPE_BASE = 10000.0
QK_PAD = 256
CONV_HALO = 16
LRU_HALO = 8
FFT_GROUP = 8
VMEM_LIMIT = 56 * 1024 * 1024


def _params(sem, vmem=VMEM_LIMIT):
    return pltpu.CompilerParams(dimension_semantics=sem, vmem_limit_bytes=vmem)


def _pick(n, candidates):
    for c in candidates:
        if n % c == 0:
            return c
    raise ValueError(f"no tile for {n} in {candidates}")


def _rms(x, g):
    return x * lax.rsqrt(jnp.mean(x * x, axis=-1, keepdims=True) + EPS) * g


def _mod_kernel(c_ref, w_ref, b_ref, o_ref):
    c = c_ref[...]
    s = c * jax.nn.sigmoid(c)
    o_ref[0] = jnp.dot(s, w_ref[0], preferred_element_type=F32,
                       precision=lax.Precision.HIGHEST) + b_ref[0]


def _modulation(cond8, mod_w, mod_b):
    depth, d, n = mod_w.shape
    tn = _pick(n, (1024, 512, 256, 128))
    return pl.pallas_call(
        _mod_kernel,
        grid=(depth, n // tn),
        in_specs=[pl.BlockSpec((8, d), lambda l, j: (0, 0)),
                  pl.BlockSpec((1, d, tn), lambda l, j: (l, 0, j)),
                  pl.BlockSpec((1, 1, tn), lambda l, j: (l, 0, j))],
        out_specs=pl.BlockSpec((1, 8, tn), lambda l, j: (l, 0, j)),
        out_shape=jax.ShapeDtypeStruct((depth, 8, n), F32),
        name="modulation",
        compiler_params=_params(("arbitrary", "arbitrary")),
    )(cond8, mod_w, mod_b.reshape(depth, 1, n))


def _norm_matmul_kernel(x_ref, m_ref, g_ref, w_ref, *rest, shift_row, scale_row):
    o_ref = rest[-1]
    gain = g_ref[...] * (1.0 + m_ref[0, scale_row:scale_row + 1, :])
    x = x_ref[0]
    h = x * lax.rsqrt(jnp.mean(x * x, axis=-1, keepdims=True) + EPS) * gain + m_ref[0, shift_row:shift_row + 1, :]
    o_ref[0] = jnp.dot(h.astype(BF16), w_ref[...], preferred_element_type=F32).astype(o_ref.dtype)


def _norm_matmul(x, mod, g, w, shift_row, scale_row, out_rows=None, row_block_off=0, into=None):
    b, l, d = x.shape
    n = w.shape[1]
    tm = _pick(l, (512, 256, 128))
    out_rows = l if out_rows is None else out_rows
    off = row_block_off
    in_specs = [pl.BlockSpec((1, tm, d), lambda bi, i: (bi, i, 0)),
                pl.BlockSpec((1, 6, d), lambda bi, i: (bi, 0, 0)),
                pl.BlockSpec((1, d), lambda bi, i: (0, 0)),
                pl.BlockSpec((d, n), lambda bi, i: (0, 0), pipeline_mode=pl.Buffered(1))]
    args = [x, mod, g.reshape(1, d), w]
    aliases = {}
    if into is not None:
        in_specs.append(pl.BlockSpec(memory_space=pl.ANY))
        args.append(into)
        aliases = {4: 0}
    return pl.pallas_call(
        functools.partial(_norm_matmul_kernel, shift_row=shift_row, scale_row=scale_row),
        grid=(b, l // tm),
        in_specs=in_specs,
        out_specs=pl.BlockSpec((1, tm, n), lambda bi, i: (bi, i + off, 0)),
        out_shape=jax.ShapeDtypeStruct((b, out_rows, n), F32),
        input_output_aliases=aliases,
        name="norm_matmul",
        compiler_params=_params(("arbitrary", "arbitrary")),
    )(*args)


def _conv_kernel(v_ref, gt_ref, vp_ref, gp_ref, vn_ref, gn_ref, w_ref, b_ref, lg_ref, lb_ref,
                 o_ref, ubuf, sh_ref, *, t, rc):
    i = pl.program_id(1)
    first = i == 0
    last = i == pl.num_programs(1) - 1
    halo = CONV_HALO
    ubuf[halo:halo + t, :] = v_ref[0] * jax.nn.sigmoid(gt_ref[0])
    up = vp_ref[0] * jax.nn.sigmoid(gp_ref[0])
    un = vn_ref[0] * jax.nn.sigmoid(gn_ref[0])
    ubuf[0:halo, :] = jnp.where(first, 0.0, up)
    ubuf[halo + t:2 * halo + t, :] = jnp.where(last, 0.0, un)
    base = halo - A_CONV // 2
    span = t + 8 * ((base + A_CONV - 1) // 8)
    for s in range(1, 8):
        sh_ref[s - 1] = ubuf[s:s + span, :]
    for r0 in range(0, t, rc):
        for g in range(N_GROUPS):
            cs = slice(g * GROUP, (g + 1) * GROUP)
            acc = jnp.zeros((rc, GROUP), F32)
            for k in range(A_CONV):
                s, a8 = (base + k) % 8, 8 * ((base + k) // 8)
                src = ubuf if s == 0 else sh_ref.at[s - 1]
                acc = acc + w_ref[k:k + 1, cs] * src[r0 + a8:r0 + a8 + rc, cs]
            y = acc + b_ref[:, cs]
            mu = jnp.mean(y, axis=-1, keepdims=True)
            dlt = y - mu
            yn = dlt * lax.rsqrt(jnp.mean(dlt * dlt, axis=-1, keepdims=True) + EPS)
            z = yn * lg_ref[:, cs] + lb_ref[:, cs]
            o_ref[0, r0:r0 + rc, cs] = (z * jax.nn.sigmoid(z)).astype(o_ref.dtype)


def _conformer_conv(p, cv_w, cv_b, ln_g, ln_b):
    b, l, _ = p.shape
    c = cv_w.shape[1]
    t = _pick(l, (256, 128))
    rc = 64
    hb = t // CONV_HALO
    nhb = l // CONV_HALO
    main = lambda col: pl.BlockSpec((1, t, c), lambda bi, i: (bi, i, col))
    prev = lambda col: pl.BlockSpec((1, CONV_HALO, c), lambda bi, i: (bi, jnp.maximum(i * hb - 1, 0), col))
    nxt = lambda col: pl.BlockSpec((1, CONV_HALO, c), lambda bi, i: (bi, jnp.minimum((i + 1) * hb, nhb - 1), col))
    vec = pl.BlockSpec((1, c), lambda bi, i: (0, 0))
    return pl.pallas_call(
        functools.partial(_conv_kernel, t=t, rc=rc),
        grid=(b, l // t),
        in_specs=[main(0), main(1), prev(0), prev(1), nxt(0), nxt(1),
                  pl.BlockSpec((A_CONV, c), lambda bi, i: (0, 0)), vec, vec, vec],
        out_specs=pl.BlockSpec((1, t, c), lambda bi, i: (bi, i, 0)),
        out_shape=jax.ShapeDtypeStruct((b, l, c), BF16),
        scratch_shapes=[pltpu.VMEM((t + 2 * CONV_HALO, c), F32),
                        pltpu.VMEM((7, t + 8 * ((CONV_HALO + A_CONV // 2) // 8), c), F32)],
        name="conformer_conv",
        compiler_params=_params(("arbitrary", "arbitrary")),
    )(p, p, p, p, p, p, cv_w, cv_b.reshape(1, c), ln_g.reshape(1, c), ln_b.reshape(1, c))


def _lru_kernel(x_ref, xh_ref, cw_ref, cb_ref, wg_ref, ba_ref, bi_ref, lam_ref, h0_ref,
                h_ref, hl_ref, xbuf, a_buf, b_buf, carry, *, t, reverse):
    i = pl.program_id(1)
    c = x_ref.shape[-1]
    halo = LRU_HALO

    @pl.when(i == 0)
    def _():
        carry[...] = h0_ref[0]

    edge = jnp.where(i == 0, 0.0, xh_ref[0])
    if reverse:
        xbuf[0:t, :] = x_ref[0]
        xbuf[t:t + halo, :] = edge
        base = 0
    else:
        xbuf[0:halo, :] = edge
        xbuf[halo:halo + t, :] = x_ref[0]
        base = halo - (RG_CONV - 1)
    xall = xbuf[...]
    n = t + halo
    y = jnp.zeros((t, c), F32) + cb_ref[...]
    for k in range(RG_CONV):
        win = xall if base + k == 0 else pltpu.roll(xall, n - (base + k), 0)
        y = y + cw_ref[k:k + 1, :] * win[:t]

    log_sig = jax.nn.log_sigmoid(lam_ref[...])
    for h in range(N_GROUPS):
        cs = slice(h * GROUP, (h + 1) * GROUP)
        yh = y[:, cs]
        gates = jnp.dot(yh.astype(BF16), wg_ref[h], preferred_element_type=F32)
        r = jax.nn.sigmoid(gates[:, :GROUP] + ba_ref[:, cs])
        ig = jax.nn.sigmoid(gates[:, GROUP:] + bi_ref[:, cs])
        log_a = RG_C * r * log_sig[:, cs]
        a = jnp.exp(log_a)
        a_buf[:, cs] = a
        b_buf[:, cs] = jnp.sqrt(1.0 - a * a) * (ig * yh)

    ngroups = t // 8
    row = lax.broadcasted_iota(jnp.int32, (8, c), 0)

    def body(gidx, cr):
        gi = (ngroups - 1 - gidx) if reverse else gidx
        r0 = pl.multiple_of(gi * 8, 8)
        a = a_buf[pl.ds(r0, 8), :]
        bb = b_buf[pl.ds(r0, 8), :]
        for k in (1, 2, 4):
            if reverse:
                a_s = pltpu.roll(a, 8 - k, 0)
                b_s = pltpu.roll(bb, 8 - k, 0)
                valid = row < 8 - k
            else:
                a_s = pltpu.roll(a, k, 0)
                b_s = pltpu.roll(bb, k, 0)
                valid = row >= k
            bb = jnp.where(valid, a * b_s + bb, bb)
            a = jnp.where(valid, a * a_s, a)
        hh = bb + a * cr
        h_ref[0, pl.ds(r0, 8), :] = hh
        edge_row = hh[0:1, :] if reverse else hh[7:8, :]
        return jnp.broadcast_to(edge_row, (8, c))

    cr = lax.fori_loop(0, ngroups, body, carry[...], unroll=4)
    carry[...] = cr
    hl_ref[0] = cr


def _rglru(p, col, h0, conv_w, conv_b, wa, ba, wi, bi, lam, reverse):
    b, l, _ = p.shape
    c = conv_w.shape[1]
    t = _pick(l, (512, 256, 128))
    nt = l // t
    hb = t // LRU_HALO
    nhb = l // LRU_HALO
    if reverse:
        tile = lambda bi_, i: (bi_, nt - 1 - i, col)
        halo = lambda bi_, i: (bi_, jnp.minimum((nt - i) * hb, nhb - 1), col)
        otile = lambda bi_, i: (bi_, nt - 1 - i, 0)
    else:
        tile = lambda bi_, i: (bi_, i, col)
        halo = lambda bi_, i: (bi_, jnp.maximum(i * hb - 1, 0), col)
        otile = lambda bi_, i: (bi_, i, 0)
    wg = jnp.concatenate([wa, wi], axis=-1).astype(BF16)
    vec = pl.BlockSpec((1, c), lambda bi_, i: (0, 0))
    state = pl.BlockSpec((1, 8, c), lambda bi_, i: (bi_, 0, 0))
    return pl.pallas_call(
        functools.partial(_lru_kernel, t=t, reverse=reverse),
        grid=(b, nt),
        in_specs=[pl.BlockSpec((1, t, c), tile),
                  pl.BlockSpec((1, LRU_HALO, c), halo),
                  pl.BlockSpec((RG_CONV, c), lambda bi_, i: (0, 0)), vec,
                  pl.BlockSpec((N_GROUPS, GROUP, 2 * GROUP), lambda bi_, i: (0, 0, 0)),
                  vec, vec, vec, state],
        out_specs=[pl.BlockSpec((1, t, c), otile), state],
        out_shape=[jax.ShapeDtypeStruct((b, l, c), F32), jax.ShapeDtypeStruct((b, 8, c), F32)],
        scratch_shapes=[pltpu.VMEM((t + LRU_HALO, c), F32), pltpu.VMEM((t, c), F32),
                        pltpu.VMEM((t, c), F32), pltpu.VMEM((8, c), F32)],
        name="rglru_rev" if reverse else "rglru_fwd",
        compiler_params=_params(("arbitrary", "arbitrary")),
    )(p, p, conv_w, conv_b.reshape(1, c), wg, ba.reshape(1, c), bi.reshape(1, c), lam.reshape(1, c), h0)


def _mixout_ab_kernel(cv_ref, hf_ref, hr_ref, pg_ref, x_ref, m_ref, g_ref, wa_ref, wb_ref, o_ref):
    rec = (hf_ref[0] + hr_ref[0]) * jax.nn.gelu(pg_ref[0])
    y = jnp.dot(cv_ref[0], wa_ref[...], preferred_element_type=F32)
    y = y + jnp.dot(rec.astype(BF16), wb_ref[...], preferred_element_type=F32)
    o_ref[0] = x_ref[0] + m_ref[0, 2:3, :] * _rms(y, g_ref[...])


def _mixout_ab(conv_out, h_f, h_r, p, x, mod, g, w_out):
    b, l, d = x.shape
    c = conv_out.shape[-1]
    tm = _pick(l, (512, 256, 128))
    half = lambda: pl.BlockSpec((1, tm, c), lambda bi, i: (bi, i, 0))
    wspec = lambda k: pl.BlockSpec((c, d), lambda bi, i: (k, 0))
    return pl.pallas_call(
        _mixout_ab_kernel,
        grid=(b, l // tm),
        in_specs=[half(), half(), half(),
                  pl.BlockSpec((1, tm, c), lambda bi, i: (bi, i, 3)),
                  pl.BlockSpec((1, tm, d), lambda bi, i: (bi, i, 0)),
                  pl.BlockSpec((1, 6, d), lambda bi, i: (bi, 0, 0)),
                  pl.BlockSpec((1, d), lambda bi, i: (0, 0)),
                  wspec(0), wspec(1)],
        out_specs=pl.BlockSpec((1, tm, d), lambda bi, i: (bi, i, 0)),
        out_shape=jax.ShapeDtypeStruct((b, l, d), F32),
        name="mixout_ab",
        compiler_params=_params(("arbitrary", "arbitrary")),
    )(conv_out, h_f, h_r, p, x, mod, g.reshape(1, d), w_out, w_out)


def _mixout_cd_kernel(f_ref, o_att_ref, x_ref, m_ref, g_ref, wa_ref, wb_ref, o_ref):
    y = jnp.dot(f_ref[0].astype(BF16), wa_ref[...], preferred_element_type=F32)
    y = y + jnp.dot(o_att_ref[0], wb_ref[...], preferred_element_type=F32)
    o_ref[0] = x_ref[0] + m_ref[0, 2:3, :] * _rms(y, g_ref[...])


def _mixout_cd(four, att, x, mod, g, w_out):
    b, l, d = x.shape
    c = four.shape[-1]
    tm = _pick(l, (512, 256, 128))
    half = lambda: pl.BlockSpec((1, tm, c), lambda bi, i: (bi, i, 0))
    wspec = lambda k: pl.BlockSpec((c, d), lambda bi, i: (k, 0))
    return pl.pallas_call(
        _mixout_cd_kernel,
        grid=(b, l // tm),
        in_specs=[half(), half(),
                  pl.BlockSpec((1, tm, d), lambda bi, i: (bi, i, 0)),
                  pl.BlockSpec((1, 6, d), lambda bi, i: (bi, 0, 0)),
                  pl.BlockSpec((1, d), lambda bi, i: (0, 0)),
                  wspec(0), wspec(1)],
        out_specs=pl.BlockSpec((1, tm, d), lambda bi, i: (bi, i, 0)),
        out_shape=jax.ShapeDtypeStruct((b, l, d), F32),
        name="mixout_cd",
        compiler_params=_params(("arbitrary", "arbitrary")),
    )(four, att, x, mod, g.reshape(1, d), w_out, w_out)


def _mlp_kernel(x_ref, m_ref, gpre_ref, gpost_ref, w1_ref, w2_ref, o_ref, h_ref, acc_ref):
    j = pl.program_id(2)
    last = pl.num_programs(2) - 1

    def chunk(h):
        u = jnp.dot(h, w1_ref[...], preferred_element_type=F32)
        u = jnp.square(jnp.maximum(u, 0.0))
        return jnp.dot(u.astype(BF16), w2_ref[...], preferred_element_type=F32)

    @pl.when(j == 0)
    def _():
        gain = gpre_ref[...] * (1.0 + m_ref[0, 4:5, :])
        x = x_ref[0]
        h = x * lax.rsqrt(jnp.mean(x * x, axis=-1, keepdims=True) + EPS) * gain + m_ref[0, 3:4, :]
        h = h.astype(BF16)
        h_ref[...] = h
        acc_ref[...] = chunk(h)

    @pl.when((j > 0) & (j < last))
    def _():
        acc_ref[...] += chunk(h_ref[...])

    @pl.when(j == last)
    def _():
        y = acc_ref[...] + chunk(h_ref[...])
        gain = m_ref[0, 5:6, :] * gpost_ref[...]
        o_ref[0] = x_ref[0] + y * lax.rsqrt(jnp.mean(y * y, axis=-1, keepdims=True) + EPS) * gain


def _mlp(x, mod, g_pre, g_post, w1, w2, layer):
    b, l, d = x.shape
    f = w1.shape[2]
    tm = _pick(l, (512, 256, 128))
    tf = _pick(f, (1024, 512, 256, 128))
    assert f // tf >= 2
    return pl.pallas_call(
        _mlp_kernel,
        grid=(b, l // tm, f // tf),
        in_specs=[pl.BlockSpec((1, tm, d), lambda bi, i, j: (bi, i, 0)),
                  pl.BlockSpec((1, 6, d), lambda bi, i, j: (bi, 0, 0)),
                  pl.BlockSpec((1, d), lambda bi, i, j: (0, 0)),
                  pl.BlockSpec((1, d), lambda bi, i, j: (0, 0)),
                  pl.BlockSpec((None, d, tf), lambda bi, i, j: (layer, 0, j)),
                  pl.BlockSpec((None, tf, d), lambda bi, i, j: (layer, j, 0))],
        out_specs=pl.BlockSpec((1, tm, d), lambda bi, i, j: (bi, i, 0)),
        out_shape=jax.ShapeDtypeStruct((b, l, d), F32),
        scratch_shapes=[pltpu.VMEM((tm, d), BF16), pltpu.VMEM((tm, d), F32)],
        name="mlp",
        compiler_params=_params(("arbitrary", "arbitrary", "arbitrary")),
    )(x, mod, g_pre.reshape(1, d), g_post.reshape(1, d), w1, w2)


def _fourier_tables(l):
    n1 = 1 << (int(math.log2(l)) // 2)
    while l % n1:
        n1 //= 2
    n2 = l // n1
    def cs(n):
        idx = np.arange(n)
        ang = 2.0 * np.pi * ((idx[:, None] * idx[None, :]) % n) / n
        return np.cos(ang), np.sin(ang)
    c1, s1 = cs(n1)
    c2, s2 = cs(n2)
    cc, sc = cs(GROUP)
    delta = 2.0 * np.pi * np.arange(n1)[:, None] / l * np.ones((1, GROUP))
    norm = 1.0 / math.sqrt(l * GROUP)
    chan = np.concatenate([cc, -sc], axis=1)
    f1 = np.concatenate([c1, s1], axis=0)
    f2 = np.concatenate([c2, s2], axis=1) * norm
    return (n1, n2, jnp.asarray(chan, BF16), jnp.asarray(f1, BF16), jnp.asarray(f2, BF16),
            jnp.asarray(np.cos(delta), F32), jnp.asarray(np.sin(delta), F32))


def _fourier_kernel(x_ref, chan_ref, f1_ref, f2_ref, dc_ref, ds_ref, o_ref, are_buf, aim_buf, tw_c, tw_s,
                    *, n1, n2):
    tw_c[...] = jnp.ones_like(tw_c)
    tw_s[...] = jnp.zeros_like(tw_s)

    def stage1(jg, _):
        j0 = jg * FFT_GROUP
        xs = jnp.concatenate([x_ref[0, pl.ds(j0 + u, n1, stride=n2), :] for u in range(FFT_GROUP)], axis=0)
        pq = jnp.dot(xs.astype(BF16), chan_ref[...], preferred_element_type=F32)
        pq = jnp.concatenate([pq[u * n1:(u + 1) * n1] for u in range(FFT_GROUP)], axis=1).astype(BF16)
        r = jnp.dot(f1_ref[...], pq, preferred_element_type=F32)
        for u in range(FFT_GROUP):
            ru = r[:, u * 2 * GROUP:(u + 1) * 2 * GROUP]
            a_re = ru[:n1, :GROUP] + ru[n1:, GROUP:]
            a_im = ru[:n1, GROUP:] - ru[n1:, :GROUP]
            tc = tw_c[...]
            ts = tw_s[...]
            row0 = pl.multiple_of((j0 + u) * n1, 8)
            are_buf[pl.ds(row0, n1), :] = a_re * tc + a_im * ts
            aim_buf[pl.ds(row0, n1), :] = a_im * tc - a_re * ts
            tw_c[...] = tc * dc_ref[...] - ts * ds_ref[...]
            tw_s[...] = ts * dc_ref[...] + tc * ds_ref[...]
        return 0

    lax.fori_loop(0, n2 // FFT_GROUP, stage1, 0)

    def stage2(kg, _):
        k0 = kg * FFT_GROUP
        rhs = jnp.concatenate(
            [jnp.concatenate([are_buf[pl.ds(k0 + u, n2, stride=n1), :], aim_buf[pl.ds(k0 + u, n2, stride=n1), :]],
                             axis=0) for u in range(FFT_GROUP)], axis=1).astype(BF16)
        y = jnp.dot(f2_ref[...], rhs, preferred_element_type=F32)
        for u in range(FFT_GROUP):
            o_ref[0, pl.ds(k0 + u, n2, stride=n1), :] = y[:, u * GROUP:(u + 1) * GROUP]
        return 0

    lax.fori_loop(0, n1 // FFT_GROUP, stage2, 0)


def _fourier_mix(p, l):
    b = p.shape[0]
    n1, n2, chan, f1, f2, dc, ds = _fourier_tables(l)
    const = lambda a: pl.BlockSpec(a.shape, lambda bi, g: (0, 0))
    return pl.pallas_call(
        functools.partial(_fourier_kernel, n1=n1, n2=n2),
        grid=(b, N_GROUPS),
        in_specs=[pl.BlockSpec((1, l, GROUP), lambda bi, g: (bi, 0, g)),
                  const(chan), const(f1), const(f2), const(dc), const(ds)],
        out_specs=pl.BlockSpec((1, l, GROUP), lambda bi, g: (bi, 0, g)),
        out_shape=jax.ShapeDtypeStruct((b, l, N_GROUPS * GROUP), F32),
        scratch_shapes=[pltpu.VMEM((l, GROUP), F32), pltpu.VMEM((l, GROUP), F32),
                        pltpu.VMEM((n1, GROUP), F32), pltpu.VMEM((n1, GROUP), F32)],
        name="fourier_mix",
        compiler_params=_params(("arbitrary", "arbitrary"), 62 * 1024 * 1024),
    )(p, chan, f1, f2, dc, ds)


def _rope_tables(l, n_ctx):
    t = np.arange(l)
    inv = ROPE_BASE ** (-np.arange(QK_ROPE // 4, dtype=np.float64) / (QK_ROPE // 4))
    ang = np.concatenate([(t // GRID_W)[:, None] * inv, (t % GRID_W)[:, None] * inv], axis=-1)
    ang = np.concatenate([ang, np.zeros((n_ctx, QK_ROPE // 2))], axis=0)
    cos, sin = np.cos(ang), np.sin(ang)
    return jnp.asarray(np.concatenate([cos, cos, -sin, sin], axis=-1), F32)


def _rope_apply(r, cs):
    tt = r * cs
    ro = tt + pltpu.roll(tt, QK_ROPE, 1)
    lane = lax.broadcasted_iota(jnp.int32, ro.shape, 1)
    return jnp.where(lane < QK_ROPE, ro, 0.0)


def _q_kernel(cq_ref, g_ref, w_ref, cs_ref, o_ref):
    n = _rms(cq_ref[0], g_ref[...]).astype(BF16)
    cs = cs_ref[...]
    for h in range(N_GROUPS):
        res = jnp.dot(n, w_ref[h], preferred_element_type=F32)
        q = jnp.concatenate([res[:, :QK_NOPE], _rope_apply(res[:, QK_NOPE:], cs)], axis=1)
        q = q * (math.log2(math.e) * QK_DIM ** -0.5)
        o_ref[0, h] = q.T.astype(o_ref.dtype)


def _mla_queries(p, l, col, q_g, w_uq, cs):
    b = p.shape[0]
    rank = w_uq.shape[0]
    w = w_uq.reshape(rank, N_GROUPS, QK_DIM)
    x1 = w[:, :, QK_NOPE:QK_NOPE + QK_ROPE // 2]
    x2 = w[:, :, QK_NOPE + QK_ROPE // 2:]
    w = jnp.concatenate([w[:, :, :QK_NOPE], x1, x2, x2, x1], axis=-1)
    w = jnp.transpose(w, (1, 0, 2)).astype(BF16)
    tm = _pick(l, (1024, 512, 256, 128))
    return pl.pallas_call(
        _q_kernel,
        grid=(b, l // tm),
        in_specs=[pl.BlockSpec((1, tm, rank), lambda bi, i: (bi, i, col)),
                  pl.BlockSpec((1, rank), lambda bi, i: (0, 0)),
                  pl.BlockSpec((N_GROUPS, rank, QK_PAD), lambda bi, i: (0, 0, 0)),
                  pl.BlockSpec((tm, 128), lambda bi, i: (i, 0))],
        out_specs=pl.BlockSpec((1, N_GROUPS, QK_PAD, tm), lambda bi, i: (bi, 0, 0, i)),
        out_shape=jax.ShapeDtypeStruct((b, N_GROUPS, QK_PAD, l), BF16),
        name="mla_q",
        compiler_params=_params(("arbitrary", "arbitrary")),
    )(p, q_g.reshape(1, rank), w, cs)


def _kv_kernel(ckv_ref, kr_ref, g_ref, w_ref, cs_ref, k_ref, v_ref):
    n = _rms(ckv_ref[0], g_ref[...]).astype(BF16)
    rope = _rope_apply(kr_ref[0], cs_ref[...]).astype(BF16)
    hw = QK_NOPE + V_DIM
    for h in range(N_GROUPS):
        res = jnp.dot(n, w_ref[:, h * hw:(h + 1) * hw], preferred_element_type=F32)
        k_ref[0, h] = jnp.concatenate([res[:, :QK_NOPE].astype(BF16), rope], axis=1)
        vt = res[:, QK_NOPE:].T.astype(BF16)
        for c in range(vt.shape[1] // 128):
            v_ref[0, h, c] = vt[:, c * 128:(c + 1) * 128]


def _mla_keys_values(p, col_kv, col_kr, kv_g, w_ukv, cs):
    b, lk, _ = p.shape
    rank = w_ukv.shape[0]
    tm = _pick(lk, (1280, 640, 256, 128))
    return pl.pallas_call(
        _kv_kernel,
        grid=(b, lk // tm),
        in_specs=[pl.BlockSpec((1, tm, rank), lambda bi, i: (bi, i, col_kv)),
                  pl.BlockSpec((1, tm, 128), lambda bi, i: (bi, i, col_kr)),
                  pl.BlockSpec((1, rank), lambda bi, i: (0, 0)),
                  pl.BlockSpec(w_ukv.shape, lambda bi, i: (0, 0)),
                  pl.BlockSpec((tm, 128), lambda bi, i: (i, 0))],
        out_specs=[pl.BlockSpec((1, N_GROUPS, tm, QK_PAD), lambda bi, i: (bi, 0, i, 0)),
                   pl.BlockSpec((1, N_GROUPS, tm // 128, V_DIM, 128), lambda bi, i: (bi, 0, i, 0, 0))],
        out_shape=[jax.ShapeDtypeStruct((b, N_GROUPS, lk, QK_PAD), BF16),
                   jax.ShapeDtypeStruct((b, N_GROUPS, lk // 128, V_DIM, 128), BF16)],
        name="mla_kv",
        compiler_params=_params(("arbitrary", "arbitrary")),
    )(p, p, kv_g.reshape(1, rank), w_ukv.astype(BF16), cs)


def _attn_kernel(qt_ref, k_ref, vt_ref, o_ref, m_ref, l_ref, acc_ref, sa_ref, sb_ref, *, tks, n_sub, rem):
    qt = qt_ref[0, 0]

    def scores(r0, width):
        return jnp.dot(k_ref[0, 0, pl.ds(r0, width), :], qt, preferred_element_type=F32)

    def update(s, c0, width):
        m_prev = m_ref[...]
        m_next = jnp.maximum(m_prev, jnp.max(s, axis=0, keepdims=True))
        p = jnp.exp2(s - pltpu.repeat(m_next, width // 8, axis=0))
        alpha = jnp.exp2(m_prev - m_next)
        l_ref[...] = alpha * l_ref[...] + jnp.sum(p, axis=0, keepdims=True)
        vt = jnp.concatenate([vt_ref[0, 0, c0 + c] for c in range(width // 128)], axis=1)
        pv = jnp.dot(vt, p.astype(BF16), preferred_element_type=F32)
        acc_ref[...] = pltpu.repeat(alpha, V_DIM // 8, axis=0) * acc_ref[...] + pv
        m_ref[...] = m_next

    def start(j):
        return pl.multiple_of(j * tks, tks)

    cpt = tks // 128
    m_ref[...] = jnp.full_like(m_ref, -jnp.inf)
    l_ref[...] = jnp.zeros_like(l_ref)
    acc_ref[...] = jnp.zeros_like(acc_ref)
    sa_ref[...] = scores(0, tks)

    def pair(i, _):
        sb_ref[...] = scores(start(2 * i + 1), tks)
        update(sa_ref[...], 2 * i * cpt, tks)
        sa_ref[...] = scores(start(2 * i + 2), tks)
        update(sb_ref[...], (2 * i + 1) * cpt, tks)
        return 0

    n_pairs = n_sub // 2
    lax.fori_loop(0, n_pairs - 1, pair, 0, unroll=5 if (n_pairs - 1) % 5 == 0 else 1)
    done = 2 * (n_pairs - 1)
    sb_ref[...] = scores(done * tks + tks, tks)
    update(sa_ref[...], done * cpt, tks)
    tail = [(j * tks, tks) for j in range(done + 2, n_sub)] + ([(n_sub * tks, rem)] if rem else [])
    bufs = [sa_ref, sb_ref]
    pending = (sb_ref, (done + 1) * tks, tks)
    for idx, (r0, width) in enumerate(tail):
        nxt = bufs[idx % 2]
        nxt[:width, :] = scores(r0, width)
        update(pending[0][:pending[2], :], pending[1] // 128, pending[2])
        pending = (nxt, r0, width)
    update(pending[0][:pending[2], :], pending[1] // 128, pending[2])
    out_t = acc_ref[...] / pltpu.repeat(l_ref[...], V_DIM // 8, axis=0)
    o_ref[0] = out_t.T.astype(o_ref.dtype)


def _attention(qt, k, vt):
    b, h, _, l = qt.shape
    lk = k.shape[2]
    tq = _pick(l, (512, 256, 128))
    tks = 512
    n_sub, rem = lk // tks, lk % tks
    assert n_sub >= 2 and rem % 128 == 0
    return pl.pallas_call(
        functools.partial(_attn_kernel, tks=tks, n_sub=n_sub, rem=rem),
        grid=(b, h, l // tq),
        in_specs=[pl.BlockSpec((1, 1, QK_PAD, tq), lambda bi, hi, i: (bi, hi, 0, i)),
                  pl.BlockSpec((1, 1, lk, QK_PAD), lambda bi, hi, i: (bi, hi, 0, 0)),
                  pl.BlockSpec((1, 1, lk // 128, V_DIM, 128), lambda bi, hi, i: (bi, hi, 0, 0, 0))],
        out_specs=pl.BlockSpec((1, tq, V_DIM), lambda bi, hi, i: (bi, i, hi)),
        out_shape=jax.ShapeDtypeStruct((b, l, h * V_DIM), BF16),
        scratch_shapes=[pltpu.VMEM((8, tq), F32), pltpu.VMEM((8, tq), F32), pltpu.VMEM((V_DIM, tq), F32),
                        pltpu.VMEM((tks, tq), F32), pltpu.VMEM((tks, tq), F32)],
        name="flash_attention",
        compiler_params=_params(("arbitrary", "arbitrary", "arbitrary")),
    )(qt, k, vt)


def _conv_lru_layer(xc, xl, ml, mc, ng, w1, w2, layer, w_in, w_out, cv_w, cv_b, cv_g, cv_beta,
                    rg_conv_w, rg_conv_b, rg_wa, rg_ba, rg_wi, rg_bi, rg_lambda, need_ctx):
    w_in = w_in.astype(BF16)
    w_out = w_out.astype(BF16)
    pc = _norm_matmul(xc, mc, ng[0], w_in, 0, 1)
    plat = _norm_matmul(xl, ml, ng[0], w_in, 0, 1)
    c = cv_w.shape[1]
    zero = jnp.zeros((xl.shape[0], 8, c), F32)
    hs_c, hs_l = [], []
    for d, reverse in enumerate((False, True)):
        args = (rg_conv_w[d], rg_conv_b[d], rg_wa[d], rg_ba[d], rg_wi[d], rg_bi[d], rg_lambda[d], reverse)
        h_c, last_c = _rglru(pc, 2, zero, *args)
        h_l, _ = _rglru(plat, 2, last_c, *args)
        hs_c.append(h_c)
        hs_l.append(h_l)

    def finish(p, hs, x, mod):
        conv_out = _conformer_conv(p, cv_w, cv_b, cv_g, cv_beta)
        x = _mixout_ab(conv_out, hs[0], hs[1], p, x, mod, ng[1], w_out)
        return _mlp(x, mod, ng[2], ng[3], w1, w2, layer)

    xl = finish(plat, hs_l, xl, ml)
    if need_ctx:
        xc = finish(pc, hs_c, xc, mc)
    return xc, xl


def _fourier_mla_layer(xc, xl, ml, mc, ng, w1, w2, layer, w_in, w_out, q_g, kv_g, w_uq, w_ukv):
    b, l, d = xl.shape
    n_ctx = xc.shape[1]
    c = N_GROUPS * GROUP
    q_rank = q_g.shape[0]
    kv_rank = kv_g.shape[0]
    kr = w_in[:, c + q_rank + kv_rank:]
    kr1, kr2 = kr[:, :QK_ROPE // 2], kr[:, QK_ROPE // 2:]
    n_used = c + q_rank + kv_rank + 2 * QK_ROPE
    n_pad = -n_used % 256
    w_in_p = jnp.concatenate([w_in[:, :c + q_rank + kv_rank], kr1, kr2, kr2, kr1,
                              jnp.zeros((d, n_pad), w_in.dtype)], axis=1).astype(BF16)
    lk = l + n_ctx
    p = _norm_matmul(xl, ml, ng[0], w_in_p, 0, 1, out_rows=lk)
    tmc = _pick(n_ctx, (512, 256, 128))
    p = _norm_matmul(xc, mc, ng[0], w_in_p, 0, 1, out_rows=lk, row_block_off=l // tmc, into=p)
    cs = _rope_tables(l, n_ctx)
    four = _fourier_mix(p, l)
    q = _mla_queries(p, l, c // q_rank, q_g, w_uq, cs)
    k, v = _mla_keys_values(p, (c + q_rank) // kv_rank, (c + q_rank + kv_rank) // 128, kv_g, w_ukv, cs)
    att = _attention(q, k, v)
    xl = _mixout_cd(four, att, xl, ml, ng[1], w_out.astype(BF16))
    return _mlp(xl, ml, ng[2], ng[3], w1, w2, layer)


def kernel(x, c, ctx, c_ctx, mod_w, mod_b, norm_g, mlp_w1, mlp_w2, ab_w_in, ab_w_out, cv_w, cv_b, cv_norm_g, cv_norm_b, rg_conv_w, rg_conv_b, rg_wa, rg_ba, rg_wi, rg_bi, rg_lambda, cd_w_in, cd_w_out, mla_q_norm_g, mla_kv_norm_g, mla_w_uq, mla_w_ukv):
    b, l, d = x.shape
    depth = mod_w.shape[0]
    assert b + 1 <= 8 and l % GRID_W == 0
    cond8 = jnp.concatenate([c, c_ctx[None, :], jnp.zeros((8 - b - 1, d), F32)], axis=0)
    mods = _modulation(cond8, mod_w, mod_b)
    w1, w2 = mlp_w1.astype(BF16), mlp_w2.astype(BF16)
    xc, xl = ctx, x
    for i in range(depth):
        need_ctx = i < depth - 1
        ml = mods[i, :b].reshape(b, 6, d)
        mc = jnp.broadcast_to(mods[i, b].reshape(1, 6, d), (b, 6, d))
        j = i // 2
        if i % 2 == 0:
            xc, xl = _conv_lru_layer(xc, xl, ml, mc, norm_g[i], w1, w2, i, ab_w_in[j], ab_w_out[j],
                                     cv_w[j], cv_b[j], cv_norm_g[j], cv_norm_b[j], rg_conv_w[j], rg_conv_b[j],
                                     rg_wa[j], rg_ba[j], rg_wi[j], rg_bi[j], rg_lambda[j], need_ctx)
        else:
            xl = _fourier_mla_layer(xc, xl, ml, mc, norm_g[i], w1, w2, i, cd_w_in[j], cd_w_out[j],
                                    mla_q_norm_g[j], mla_kv_norm_g[j], mla_w_uq[j], mla_w_ukv[j])
    return xl
```

```python
import functools
import math

import numpy as np
import jax
import jax.numpy as jnp
from jax import lax
from jax.experimental import pallas as pl
from jax.experimental.pallas import tpu as pltpu

F32 = jnp.float32
BF16 = jnp.bfloat16

EPS = 1e-6
GRID_W = 64
N_GROUPS = 8
GROUP = 128
A_CONV = 31
RG_CONV = 4
RG_C = 8.0
QK_NOPE = 128
QK_ROPE = 64
QK_DIM = QK_NOPE + QK_ROPE
V_DIM = 128
ROPE_BASE = 10000.0
QK_PAD = 256
CONV_HALO = 16
LRU_HALO = 8
FFT_GROUP = 16
VMEM_LIMIT = 56 * 1024 * 1024


def _params(sem, vmem=VMEM_LIMIT):
    return pltpu.CompilerParams(dimension_semantics=sem, vmem_limit_bytes=vmem)


def _pick(n, candidates):
    for c in candidates:
        if n % c == 0:
            return c
    raise ValueError(f"no tile for {n} in {candidates}")


def _rms(x, g):
    return x * lax.rsqrt(jnp.mean(x * x, axis=-1, keepdims=True) + EPS) * g


def _mod_kernel(c_ref, w_ref, b_ref, o_ref):
    c = c_ref[...]
    s = c * jax.nn.sigmoid(c)
    o_ref[0] = jnp.dot(s, w_ref[0], preferred_element_type=F32,
                       precision=lax.Precision.HIGHEST) + b_ref[0]


def _modulation(cond8, mod_w, mod_b):
    depth, d, n = mod_w.shape
    tn = _pick(n, (2048, 1024, 512, 256, 128))
    return pl.pallas_call(
        _mod_kernel,
        grid=(depth, n // tn),
        in_specs=[pl.BlockSpec((8, d), lambda l, j: (0, 0)),
                  pl.BlockSpec((1, d, tn), lambda l, j: (l, 0, j)),
                  pl.BlockSpec((1, 1, tn), lambda l, j: (l, 0, j))],
        out_specs=pl.BlockSpec((1, 8, tn), lambda l, j: (l, 0, j)),
        out_shape=jax.ShapeDtypeStruct((depth, 8, n), F32),
        name="modulation",
        compiler_params=_params(("arbitrary", "arbitrary")),
    )(cond8, mod_w, mod_b.reshape(depth, 1, n))


def _norm_matmul_kernel(x_ref, m_ref, g_ref, w_ref, *rest, shift_row, scale_row):
    o_ref = rest[-1]
    gain = g_ref[...] * (1.0 + m_ref[0, scale_row:scale_row + 1, :])
    x = x_ref[0]
    h = x * lax.rsqrt(jnp.mean(x * x, axis=-1, keepdims=True) + EPS) * gain + m_ref[0, shift_row:shift_row + 1, :]
    o_ref[0] = jnp.dot(h.astype(BF16), w_ref[...], preferred_element_type=F32).astype(o_ref.dtype)


def _norm_matmul(x, mod, g, w, shift_row, scale_row, out_rows=None, row_block_off=0, into=None):
    b, l, d = x.shape
    n = w.shape[1]
    tm = _pick(l, (512, 256, 128))
    out_rows = l if out_rows is None else out_rows
    off = row_block_off
    in_specs = [pl.BlockSpec((1, tm, d), lambda bi, i: (bi, i, 0)),
                pl.BlockSpec((1, 6, d), lambda bi, i: (bi, 0, 0)),
                pl.BlockSpec((1, d), lambda bi, i: (0, 0)),
                pl.BlockSpec((d, n), lambda bi, i: (0, 0), pipeline_mode=pl.Buffered(1))]
    args = [x, mod, g.reshape(1, d), w]
    aliases = {}
    if into is not None:
        in_specs.append(pl.BlockSpec(memory_space=pl.ANY))
        args.append(into)
        aliases = {4: 0}
    return pl.pallas_call(
        functools.partial(_norm_matmul_kernel, shift_row=shift_row, scale_row=scale_row),
        grid=(b, l // tm),
        in_specs=in_specs,
        out_specs=pl.BlockSpec((1, tm, n), lambda bi, i: (bi, i + off, 0)),
        out_shape=jax.ShapeDtypeStruct((b, out_rows, n), F32),
        input_output_aliases=aliases,
        name="norm_matmul",
        compiler_params=_params(("arbitrary", "arbitrary")),
    )(*args)


def _conv_kernel(v_ref, gt_ref, vp_ref, gp_ref, vn_ref, gn_ref, w_ref, b_ref, lg_ref, lb_ref,
                 o_ref, ubuf, sh_ref, *, t, rc):
    i = pl.program_id(1)
    first = i == 0
    last = i == pl.num_programs(1) - 1
    halo = CONV_HALO
    ubuf[halo:halo + t, :] = v_ref[0] * jax.nn.sigmoid(gt_ref[0])
    up = vp_ref[0] * jax.nn.sigmoid(gp_ref[0])
    un = vn_ref[0] * jax.nn.sigmoid(gn_ref[0])
    ubuf[0:halo, :] = jnp.where(first, 0.0, up)
    ubuf[halo + t:2 * halo + t, :] = jnp.where(last, 0.0, un)
    base = halo - A_CONV // 2
    span = t + 8 * ((base + A_CONV - 1) // 8)
    for s in range(1, 8):
        sh_ref[s - 1] = ubuf[s:s + span, :]
    for r0 in range(0, t, rc):
        for g in range(N_GROUPS):
            cs = slice(g * GROUP, (g + 1) * GROUP)
            acc = jnp.zeros((rc, GROUP), F32)
            for k in range(A_CONV):
                s, a8 = (base + k) % 8, 8 * ((base + k) // 8)
                src = ubuf if s == 0 else sh_ref.at[s - 1]
                acc = acc + w_ref[k:k + 1, cs] * src[r0 + a8:r0 + a8 + rc, cs]
            y = acc + b_ref[:, cs]
            mu = jnp.mean(y, axis=-1, keepdims=True)
            dlt = y - mu
            yn = dlt * lax.rsqrt(jnp.mean(dlt * dlt, axis=-1, keepdims=True) + EPS)
            z = yn * lg_ref[:, cs] + lb_ref[:, cs]
            o_ref[0, r0:r0 + rc, cs] = (z * jax.nn.sigmoid(z)).astype(o_ref.dtype)


def _conformer_conv(p, cv_w, cv_b, ln_g, ln_b):
    b, l, _ = p.shape
    c = cv_w.shape[1]
    t = _pick(l, (512, 256, 128))
    rc = 64
    hb = t // CONV_HALO
    nhb = l // CONV_HALO
    main = lambda col: pl.BlockSpec((1, t, c), lambda bi, i: (bi, i, col))
    prev = lambda col: pl.BlockSpec((1, CONV_HALO, c), lambda bi, i: (bi, jnp.maximum(i * hb - 1, 0), col))
    nxt = lambda col: pl.BlockSpec((1, CONV_HALO, c), lambda bi, i: (bi, jnp.minimum((i + 1) * hb, nhb - 1), col))
    vec = pl.BlockSpec((1, c), lambda bi, i: (0, 0))
    return pl.pallas_call(
        functools.partial(_conv_kernel, t=t, rc=rc),
        grid=(b, l // t),
        in_specs=[main(0), main(1), prev(0), prev(1), nxt(0), nxt(1),
                  pl.BlockSpec((A_CONV, c), lambda bi, i: (0, 0)), vec, vec, vec],
        out_specs=pl.BlockSpec((1, t, c), lambda bi, i: (bi, i, 0)),
        out_shape=jax.ShapeDtypeStruct((b, l, c), BF16),
        scratch_shapes=[pltpu.VMEM((t + 2 * CONV_HALO, c), F32),
                        pltpu.VMEM((7, t + 8 * ((CONV_HALO + A_CONV // 2) // 8), c), F32)],
        name="conformer_conv",
        compiler_params=_params(("arbitrary", "arbitrary")),
    )(p, p, p, p, p, p, cv_w, cv_b.reshape(1, c), ln_g.reshape(1, c), ln_b.reshape(1, c))


def _lru_kernel(x_ref, xh_ref, cw_ref, cb_ref, wg_ref, ba_ref, bi_ref, lam_ref, h0_ref,
                h_ref, hl_ref, xbuf, a_buf, b_buf, carry, *, t, reverse):
    i = pl.program_id(1)
    c = x_ref.shape[-1]
    halo = LRU_HALO

    @pl.when(i == 0)
    def _():
        carry[...] = h0_ref[0]

    edge = jnp.where(i == 0, 0.0, xh_ref[0])
    if reverse:
        xbuf[0:t, :] = x_ref[0]
        xbuf[t:t + halo, :] = edge
        base = 0
    else:
        xbuf[0:halo, :] = edge
        xbuf[halo:halo + t, :] = x_ref[0]
        base = halo - (RG_CONV - 1)
    xall = xbuf[...]
    n = t + halo
    y = jnp.zeros((t, c), F32) + cb_ref[...]
    for k in range(RG_CONV):
        win = xall if base + k == 0 else pltpu.roll(xall, n - (base + k), 0)
        y = y + cw_ref[k:k + 1, :] * win[:t]

    log_sig = jax.nn.log_sigmoid(lam_ref[...])
    for h in range(N_GROUPS):
        cs = slice(h * GROUP, (h + 1) * GROUP)
        yh = y[:, cs]
        gates = jnp.dot(yh.astype(BF16), wg_ref[h], preferred_element_type=F32)
        r = jax.nn.sigmoid(gates[:, :GROUP] + ba_ref[:, cs])
        ig = jax.nn.sigmoid(gates[:, GROUP:] + bi_ref[:, cs])
        log_a = RG_C * r * log_sig[:, cs]
        a = jnp.exp(log_a)
        a_buf[:, cs] = a
        b_buf[:, cs] = jnp.sqrt(1.0 - a * a) * (ig * yh)

    ngroups = t // 8
    row = lax.broadcasted_iota(jnp.int32, (8, c), 0)

    def body(gidx, cr):
        gi = (ngroups - 1 - gidx) if reverse else gidx
        r0 = pl.multiple_of(gi * 8, 8)
        a = a_buf[pl.ds(r0, 8), :]
        bb = b_buf[pl.ds(r0, 8), :]
        for k in (1, 2, 4):
            if reverse:
                a_s = pltpu.roll(a, 8 - k, 0)
                b_s = pltpu.roll(bb, 8 - k, 0)
                valid = row < 8 - k
            else:
                a_s = pltpu.roll(a, k, 0)
                b_s = pltpu.roll(bb, k, 0)
                valid = row >= k
            bb = jnp.where(valid, a * b_s + bb, bb)
            a = jnp.where(valid, a * a_s, a)
        hh = bb + a * cr
        h_ref[0, pl.ds(r0, 8), :] = hh
        edge_row = hh[0:1, :] if reverse else hh[7:8, :]
        return jnp.broadcast_to(edge_row, (8, c))

    cr = lax.fori_loop(0, ngroups, body, carry[...], unroll=4)
    carry[...] = cr
    hl_ref[0] = cr


def _rglru(p, col, h0, conv_w, conv_b, wa, ba, wi, bi, lam, reverse):
    b, l, _ = p.shape
    c = conv_w.shape[1]
    t = _pick(l, (1024, 512, 256, 128))
    nt = l // t
    hb = t // LRU_HALO
    nhb = l // LRU_HALO
    if reverse:
        tile = lambda bi_, i: (bi_, nt - 1 - i, col)
        halo = lambda bi_, i: (bi_, jnp.minimum((nt - i) * hb, nhb - 1), col)
        otile = lambda bi_, i: (bi_, nt - 1 - i, 0)
    else:
        tile = lambda bi_, i: (bi_, i, col)
        halo = lambda bi_, i: (bi_, jnp.maximum(i * hb - 1, 0), col)
        otile = lambda bi_, i: (bi_, i, 0)
    wg = jnp.concatenate([wa, wi], axis=-1).astype(BF16)
    vec = pl.BlockSpec((1, c), lambda bi_, i: (0, 0))
    state = pl.BlockSpec((1, 8, c), lambda bi_, i: (bi_, 0, 0))
    return pl.pallas_call(
        functools.partial(_lru_kernel, t=t, reverse=reverse),
        grid=(b, nt),
        in_specs=[pl.BlockSpec((1, t, c), tile),
                  pl.BlockSpec((1, LRU_HALO, c), halo),
                  pl.BlockSpec((RG_CONV, c), lambda bi_, i: (0, 0)), vec,
                  pl.BlockSpec((N_GROUPS, GROUP, 2 * GROUP), lambda bi_, i: (0, 0, 0)),
                  vec, vec, vec, state],
        out_specs=[pl.BlockSpec((1, t, c), otile), state],
        out_shape=[jax.ShapeDtypeStruct((b, l, c), F32), jax.ShapeDtypeStruct((b, 8, c), F32)],
        scratch_shapes=[pltpu.VMEM((t + LRU_HALO, c), F32), pltpu.VMEM((t, c), F32),
                        pltpu.VMEM((t, c), F32), pltpu.VMEM((8, c), F32)],
        name="rglru_rev" if reverse else "rglru_fwd",
        compiler_params=_params(("arbitrary", "arbitrary")),
    )(p, p, conv_w, conv_b.reshape(1, c), wg, ba.reshape(1, c), bi.reshape(1, c), lam.reshape(1, c), h0)


def _mixout_ab_kernel(cv_ref, hf_ref, hr_ref, pg_ref, x_ref, m_ref, g_ref, wa_ref, wb_ref, o_ref):
    rec = (hf_ref[0] + hr_ref[0]) * jax.nn.gelu(pg_ref[0])
    y = jnp.dot(cv_ref[0], wa_ref[...], preferred_element_type=F32)
    y = y + jnp.dot(rec.astype(BF16), wb_ref[...], preferred_element_type=F32)
    o_ref[0] = x_ref[0] + m_ref[0, 2:3, :] * _rms(y, g_ref[...])


def _mixout_ab(conv_out, h_f, h_r, p, x, mod, g, w_out):
    b, l, d = x.shape
    c = conv_out.shape[-1]
    tm = _pick(l, (512, 256, 128))
    half = lambda: pl.BlockSpec((1, tm, c), lambda bi, i: (bi, i, 0))
    wspec = lambda k: pl.BlockSpec((c, d), lambda bi, i: (k, 0))
    return pl.pallas_call(
        _mixout_ab_kernel,
        grid=(b, l // tm),
        in_specs=[half(), half(), half(),
                  pl.BlockSpec((1, tm, c), lambda bi, i: (bi, i, 3)),
                  pl.BlockSpec((1, tm, d), lambda bi, i: (bi, i, 0)),
                  pl.BlockSpec((1, 6, d), lambda bi, i: (bi, 0, 0)),
                  pl.BlockSpec((1, d), lambda bi, i: (0, 0)),
                  wspec(0), wspec(1)],
        out_specs=pl.BlockSpec((1, tm, d), lambda bi, i: (bi, i, 0)),
        out_shape=jax.ShapeDtypeStruct((b, l, d), F32),
        name="mixout_ab",
        compiler_params=_params(("arbitrary", "arbitrary")),
    )(conv_out, h_f, h_r, p, x, mod, g.reshape(1, d), w_out, w_out)


def _mixout_cd_kernel(f_ref, o_att_ref, x_ref, m_ref, g_ref, wa_ref, wb_ref, o_ref):
    y = jnp.dot(f_ref[0].astype(BF16), wa_ref[...], preferred_element_type=F32)
    y = y + jnp.dot(o_att_ref[0], wb_ref[...], preferred_element_type=F32)
    o_ref[0] = x_ref[0] + m_ref[0, 2:3, :] * _rms(y, g_ref[...])


def _mixout_cd(four, att, x, mod, g, w_out):
    b, l, d = x.shape
    c = four.shape[-1]
    tm = _pick(l, (512, 256, 128))
    half = lambda: pl.BlockSpec((1, tm, c), lambda bi, i: (bi, i, 0))
    wspec = lambda k: pl.BlockSpec((c, d), lambda bi, i: (k, 0))
    return pl.pallas_call(
        _mixout_cd_kernel,
        grid=(b, l // tm),
        in_specs=[half(), half(),
                  pl.BlockSpec((1, tm, d), lambda bi, i: (bi, i, 0)),
                  pl.BlockSpec((1, 6, d), lambda bi, i: (bi, 0, 0)),
                  pl.BlockSpec((1, d), lambda bi, i: (0, 0)),
                  wspec(0), wspec(1)],
        out_specs=pl.BlockSpec((1, tm, d), lambda bi, i: (bi, i, 0)),
        out_shape=jax.ShapeDtypeStruct((b, l, d), F32),
        name="mixout_cd",
        compiler_params=_params(("arbitrary", "arbitrary")),
    )(four, att, x, mod, g.reshape(1, d), w_out, w_out)


def _mlp_kernel(x_ref, m_ref, gpre_ref, gpost_ref, w1_ref, w2_ref, o_ref, h_ref, acc_ref):
    j = pl.program_id(2)
    last = pl.num_programs(2) - 1

    def chunk(h):
        u = jnp.dot(h, w1_ref[...], preferred_element_type=F32)
        u = jnp.square(jnp.maximum(u, 0.0))
        return jnp.dot(u.astype(BF16), w2_ref[...], preferred_element_type=F32)

    @pl.when(j == 0)
    def _():
        gain = gpre_ref[...] * (1.0 + m_ref[0, 4:5, :])
        x = x_ref[0]
        h = x * lax.rsqrt(jnp.mean(x * x, axis=-1, keepdims=True) + EPS) * gain + m_ref[0, 3:4, :]
        h = h.astype(BF16)
        h_ref[...] = h
        acc_ref[...] = chunk(h)

    @pl.when((j > 0) & (j < last))
    def _():
        acc_ref[...] += chunk(h_ref[...])

    @pl.when(j == last)
    def _():
        y = acc_ref[...] + chunk(h_ref[...])
        gain = m_ref[0, 5:6, :] * gpost_ref[...]
        o_ref[0] = x_ref[0] + y * lax.rsqrt(jnp.mean(y * y, axis=-1, keepdims=True) + EPS) * gain


def _mlp(x, mod, g_pre, g_post, w1, w2, layer):
    b, l, d = x.shape
    f = w1.shape[2]
    tm = _pick(l, (512, 256, 128))
    tf = _pick(f, (1024, 512, 256, 128))
    assert f // tf >= 2
    return pl.pallas_call(
        _mlp_kernel,
        grid=(b, l // tm, f // tf),
        in_specs=[pl.BlockSpec((1, tm, d), lambda bi, i, j: (bi, i, 0)),
                  pl.BlockSpec((1, 6, d), lambda bi, i, j: (bi, 0, 0)),
                  pl.BlockSpec((1, d), lambda bi, i, j: (0, 0)),
                  pl.BlockSpec((1, d), lambda bi, i, j: (0, 0)),
                  pl.BlockSpec((None, d, tf), lambda bi, i, j: (layer, 0, j)),
                  pl.BlockSpec((None, tf, d), lambda bi, i, j: (layer, j, 0))],
        out_specs=pl.BlockSpec((1, tm, d), lambda bi, i, j: (bi, i, 0)),
        out_shape=jax.ShapeDtypeStruct((b, l, d), F32),
        scratch_shapes=[pltpu.VMEM((tm, d), BF16), pltpu.VMEM((tm, d), F32)],
        name="mlp",
        compiler_params=_params(("arbitrary", "arbitrary", "arbitrary")),
    )(x, mod, g_pre.reshape(1, d), g_post.reshape(1, d), w1, w2)


def _fourier_tables(l):
    n1 = 1 << (int(math.log2(l)) // 2)
    while l % n1:
        n1 //= 2
    n2 = l // n1
    def cs(n):
        idx = np.arange(n)
        ang = 2.0 * np.pi * ((idx[:, None] * idx[None, :]) % n) / n
        return np.cos(ang), np.sin(ang)
    c1, s1 = cs(n1)
    c2, s2 = cs(n2)
    cc, sc = cs(GROUP)
    delta = 2.0 * np.pi * np.arange(n1)[:, None] / l * np.ones((1, GROUP))
    norm = 1.0 / math.sqrt(l * GROUP)
    chan = np.concatenate([cc, -sc], axis=1)
    f1 = np.concatenate([c1, s1], axis=0)
    f2 = np.concatenate([c2, s2], axis=1) * norm
    return (n1, n2, jnp.asarray(chan, BF16), jnp.asarray(f1, BF16), jnp.asarray(f2, BF16),
            jnp.asarray(np.cos(delta), F32), jnp.asarray(np.sin(delta), F32))


def _fourier_kernel(x_ref, chan_ref, f1_ref, f2_ref, dc_ref, ds_ref, o_ref, are_buf, aim_buf, tw_c, tw_s,
                    *, n1, n2):
    tw_c[...] = jnp.ones_like(tw_c)
    tw_s[...] = jnp.zeros_like(tw_s)

    def stage1(jg, _):
        j0 = jg * FFT_GROUP
        xs = jnp.concatenate([x_ref[0, pl.ds(j0 + u, n1, stride=n2), :] for u in range(FFT_GROUP)], axis=0)
        pq = jnp.dot(xs.astype(BF16), chan_ref[...], preferred_element_type=F32)
        pq = jnp.concatenate([pq[u * n1:(u + 1) * n1] for u in range(FFT_GROUP)], axis=1).astype(BF16)
        r = jnp.dot(f1_ref[...], pq, preferred_element_type=F32)
        for u in range(FFT_GROUP):
            ru = r[:, u * 2 * GROUP:(u + 1) * 2 * GROUP]
            a_re = ru[:n1, :GROUP] + ru[n1:, GROUP:]
            a_im = ru[:n1, GROUP:] - ru[n1:, :GROUP]
            tc = tw_c[...]
            ts = tw_s[...]
            row0 = pl.multiple_of((j0 + u) * n1, 8)
            are_buf[pl.ds(row0, n1), :] = a_re * tc + a_im * ts
            aim_buf[pl.ds(row0, n1), :] = a_im * tc - a_re * ts
            tw_c[...] = tc * dc_ref[...] - ts * ds_ref[...]
            tw_s[...] = ts * dc_ref[...] + tc * ds_ref[...]
        return 0

    lax.fori_loop(0, n2 // FFT_GROUP, stage1, 0)

    def stage2(kg, _):
        k0 = kg * FFT_GROUP
        rhs = jnp.concatenate(
            [jnp.concatenate([are_buf[pl.ds(k0 + u, n2, stride=n1), :], aim_buf[pl.ds(k0 + u, n2, stride=n1), :]],
                             axis=0) for u in range(FFT_GROUP)], axis=1).astype(BF16)
        y = jnp.dot(f2_ref[...], rhs, preferred_element_type=F32)
        for u in range(FFT_GROUP):
            o_ref[0, pl.ds(k0 + u, n2, stride=n1), :] = y[:, u * GROUP:(u + 1) * GROUP]
        return 0

    lax.fori_loop(0, n1 // FFT_GROUP, stage2, 0)


def _fourier_mix(p, l):
    b = p.shape[0]
    n1, n2, chan, f1, f2, dc, ds = _fourier_tables(l)
    const = lambda a: pl.BlockSpec(a.shape, lambda bi, g: (0, 0))
    return pl.pallas_call(
        functools.partial(_fourier_kernel, n1=n1, n2=n2),
        grid=(b, N_GROUPS),
        in_specs=[pl.BlockSpec((1, l, GROUP), lambda bi, g: (bi, 0, g)),
                  const(chan), const(f1), const(f2), const(dc), const(ds)],
        out_specs=pl.BlockSpec((1, l, GROUP), lambda bi, g: (bi, 0, g)),
        out_shape=jax.ShapeDtypeStruct((b, l, N_GROUPS * GROUP), F32),
        scratch_shapes=[pltpu.VMEM((l, GROUP), F32), pltpu.VMEM((l, GROUP), F32),
                        pltpu.VMEM((n1, GROUP), F32), pltpu.VMEM((n1, GROUP), F32)],
        name="fourier_mix",
        compiler_params=_params(("arbitrary", "arbitrary"), 62 * 1024 * 1024),
    )(p, chan, f1, f2, dc, ds)


def _rope_tables(l, n_ctx):
    t = np.arange(l)
    inv = ROPE_BASE ** (-np.arange(QK_ROPE // 4, dtype=np.float64) / (QK_ROPE // 4))
    ang = np.concatenate([(t // GRID_W)[:, None] * inv, (t % GRID_W)[:, None] * inv], axis=-1)
    ang = np.concatenate([ang, np.zeros((n_ctx, QK_ROPE // 2))], axis=0)
    cos, sin = np.cos(ang), np.sin(ang)
    return jnp.asarray(np.concatenate([cos, cos, -sin, sin], axis=-1), F32)


def _rope_apply(r, cs):
    tt = r * cs
    ro = tt + pltpu.roll(tt, QK_ROPE, 1)
    lane = lax.broadcasted_iota(jnp.int32, ro.shape, 1)
    return jnp.where(lane < QK_ROPE, ro, 0.0)


def _q_kernel(cq_ref, g_ref, w_ref, cs_ref, o_ref):
    n = _rms(cq_ref[0], g_ref[...]).astype(BF16)
    cs = cs_ref[...]
    for h in range(N_GROUPS):
        res = jnp.dot(n, w_ref[h], preferred_element_type=F32)
        q = jnp.concatenate([res[:, :QK_NOPE], _rope_apply(res[:, QK_NOPE:], cs)], axis=1)
        q = q * (math.log2(math.e) * QK_DIM ** -0.5)
        o_ref[0, h] = q.T.astype(o_ref.dtype)


def _mla_queries(p, l, col, q_g, w_uq, cs):
    b = p.shape[0]
    rank = w_uq.shape[0]
    w = w_uq.reshape(rank, N_GROUPS, QK_DIM)
    x1 = w[:, :, QK_NOPE:QK_NOPE + QK_ROPE // 2]
    x2 = w[:, :, QK_NOPE + QK_ROPE // 2:]
    w = jnp.concatenate([w[:, :, :QK_NOPE], x1, x2, x2, x1], axis=-1)
    w = jnp.transpose(w, (1, 0, 2)).astype(BF16)
    tm = _pick(l, (1024, 512, 256, 128))
    return pl.pallas_call(
        _q_kernel,
        grid=(b, l // tm),
        in_specs=[pl.BlockSpec((1, tm, rank), lambda bi, i: (bi, i, col)),
                  pl.BlockSpec((1, rank), lambda bi, i: (0, 0)),
                  pl.BlockSpec((N_GROUPS, rank, QK_PAD), lambda bi, i: (0, 0, 0)),
                  pl.BlockSpec((tm, 128), lambda bi, i: (i, 0))],
        out_specs=pl.BlockSpec((1, N_GROUPS, QK_PAD, tm), lambda bi, i: (bi, 0, 0, i)),
        out_shape=jax.ShapeDtypeStruct((b, N_GROUPS, QK_PAD, l), BF16),
        name="mla_q",
        compiler_params=_params(("arbitrary", "arbitrary")),
    )(p, q_g.reshape(1, rank), w, cs)


def _kv_kernel(ckv_ref, kr_ref, g_ref, w_ref, cs_ref, k_ref, v_ref):
    n = _rms(ckv_ref[0], g_ref[...]).astype(BF16)
    rope = _rope_apply(kr_ref[0], cs_ref[...]).astype(BF16)
    hw = QK_NOPE + V_DIM
    for h in range(N_GROUPS):
        res = jnp.dot(n, w_ref[:, h * hw:(h + 1) * hw], preferred_element_type=F32)
        k_ref[0, h] = jnp.concatenate([res[:, :QK_NOPE].astype(BF16), rope], axis=1)
        vt = res[:, QK_NOPE:].T.astype(BF16)
        for c in range(vt.shape[1] // 128):
            v_ref[0, h, c] = vt[:, c * 128:(c + 1) * 128]


def _mla_keys_values(p, col_kv, col_kr, kv_g, w_ukv, cs):
    b, lk, _ = p.shape
    rank = w_ukv.shape[0]
    tm = _pick(lk, (1280, 640, 256, 128))
    return pl.pallas_call(
        _kv_kernel,
        grid=(b, lk // tm),
        in_specs=[pl.BlockSpec((1, tm, rank), lambda bi, i: (bi, i, col_kv)),
                  pl.BlockSpec((1, tm, 128), lambda bi, i: (bi, i, col_kr)),
                  pl.BlockSpec((1, rank), lambda bi, i: (0, 0)),
                  pl.BlockSpec(w_ukv.shape, lambda bi, i: (0, 0)),
                  pl.BlockSpec((tm, 128), lambda bi, i: (i, 0))],
        out_specs=[pl.BlockSpec((1, N_GROUPS, tm, QK_PAD), lambda bi, i: (bi, 0, i, 0)),
                   pl.BlockSpec((1, N_GROUPS, tm // 128, V_DIM, 128), lambda bi, i: (bi, 0, i, 0, 0))],
        out_shape=[jax.ShapeDtypeStruct((b, N_GROUPS, lk, QK_PAD), BF16),
                   jax.ShapeDtypeStruct((b, N_GROUPS, lk // 128, V_DIM, 128), BF16)],
        name="mla_kv",
        compiler_params=_params(("arbitrary", "arbitrary")),
    )(p, p, kv_g.reshape(1, rank), w_ukv.astype(BF16), cs)


def _attn_kernel(qt_ref, k_ref, vt_ref, o_ref, m_ref, l_ref, acc_ref, sa_ref, sb_ref, *, tks, n_sub, rem):
    qt = qt_ref[0, 0]

    def scores(r0, width):
        return jnp.dot(k_ref[0, 0, pl.ds(r0, width), :], qt, preferred_element_type=F32)

    def update(s, c0, width):
        m_prev = m_ref[...]
        m_next = jnp.maximum(m_prev, jnp.max(s, axis=0, keepdims=True))
        p = jnp.exp2(s - pltpu.repeat(m_next, width // 8, axis=0))
        alpha = jnp.exp2(m_prev - m_next)
        l_ref[...] = alpha * l_ref[...] + jnp.sum(p, axis=0, keepdims=True)
        vt = jnp.concatenate([vt_ref[0, 0, c0 + c] for c in range(width // 128)], axis=1)
        pv = jnp.dot(vt, p.astype(BF16), preferred_element_type=F32)
        acc_ref[...] = pltpu.repeat(alpha, V_DIM // 8, axis=0) * acc_ref[...] + pv
        m_ref[...] = m_next

    def start(j):
        return pl.multiple_of(j * tks, tks)

    cpt = tks // 128
    m_ref[...] = jnp.full_like(m_ref, -jnp.inf)
    l_ref[...] = jnp.zeros_like(l_ref)
    acc_ref[...] = jnp.zeros_like(acc_ref)
    sa_ref[...] = scores(0, tks)

    def pair(i, _):
        sb_ref[...] = scores(start(2 * i + 1), tks)
        update(sa_ref[...], 2 * i * cpt, tks)
        sa_ref[...] = scores(start(2 * i + 2), tks)
        update(sb_ref[...], (2 * i + 1) * cpt, tks)
        return 0

    n_pairs = n_sub // 2
    lax.fori_loop(0, n_pairs - 1, pair, 0, unroll=5 if (n_pairs - 1) % 5 == 0 else 1)
    done = 2 * (n_pairs - 1)
    sb_ref[...] = scores(done * tks + tks, tks)
    update(sa_ref[...], done * cpt, tks)
    tail = [(j * tks, tks) for j in range(done + 2, n_sub)] + ([(n_sub * tks, rem)] if rem else [])
    bufs = [sa_ref, sb_ref]
    pending = (sb_ref, (done + 1) * tks, tks)
    for idx, (r0, width) in enumerate(tail):
        nxt = bufs[idx % 2]
        nxt[:width, :] = scores(r0, width)
        update(pending[0][:pending[2], :], pending[1] // 128, pending[2])
        pending = (nxt, r0, width)
    update(pending[0][:pending[2], :], pending[1] // 128, pending[2])
    out_t = acc_ref[...] / pltpu.repeat(l_ref[...], V_DIM // 8, axis=0)
    o_ref[0] = out_t.T.astype(o_ref.dtype)


def _attention(qt, k, vt):
    b, h, _, l = qt.shape
    lk = k.shape[2]
    tq = _pick(l, (1024, 512, 256, 128))
    tks = 512
    n_sub, rem = lk // tks, lk % tks
    assert n_sub >= 2 and rem % 128 == 0
    return pl.pallas_call(
        functools.partial(_attn_kernel, tks=tks, n_sub=n_sub, rem=rem),
        grid=(b, h, l // tq),
        in_specs=[pl.BlockSpec((1, 1, QK_PAD, tq), lambda bi, hi, i: (bi, hi, 0, i)),
                  pl.BlockSpec((1, 1, lk, QK_PAD), lambda bi, hi, i: (bi, hi, 0, 0)),
                  pl.BlockSpec((1, 1, lk // 128, V_DIM, 128), lambda bi, hi, i: (bi, hi, 0, 0, 0))],
        out_specs=pl.BlockSpec((1, tq, V_DIM), lambda bi, hi, i: (bi, i, hi)),
        out_shape=jax.ShapeDtypeStruct((b, l, h * V_DIM), BF16),
        scratch_shapes=[pltpu.VMEM((8, tq), F32), pltpu.VMEM((8, tq), F32), pltpu.VMEM((V_DIM, tq), F32),
                        pltpu.VMEM((tks, tq), F32), pltpu.VMEM((tks, tq), F32)],
        name="flash_attention",
        compiler_params=_params(("arbitrary", "arbitrary", "arbitrary")),
    )(qt, k, vt)


def _conv_lru_layer(xc, xl, ml, mc, ng, w1, w2, layer, w_in, w_out, cv_w, cv_b, cv_g, cv_beta,
                    rg_conv_w, rg_conv_b, rg_wa, rg_ba, rg_wi, rg_bi, rg_lambda, need_ctx):
    w_in = w_in.astype(BF16)
    w_out = w_out.astype(BF16)
    pc = _norm_matmul(xc, mc, ng[0], w_in, 0, 1)
    plat = _norm_matmul(xl, ml, ng[0], w_in, 0, 1)
    c = cv_w.shape[1]
    zero = jnp.zeros((xl.shape[0], 8, c), F32)
    hs_c, hs_l = [], []
    for d, reverse in enumerate((False, True)):
        args = (rg_conv_w[d], rg_conv_b[d], rg_wa[d], rg_ba[d], rg_wi[d], rg_bi[d], rg_lambda[d], reverse)
        h_c, last_c = _rglru(pc, 2, zero, *args)
        h_l, _ = _rglru(plat, 2, last_c, *args)
        hs_c.append(h_c)
        hs_l.append(h_l)

    def finish(p, hs, x, mod, shared_mod):
        conv_out = _conformer_conv(p, cv_w, cv_b, cv_g, cv_beta)
        x = _mixout_ab(conv_out, hs[0], hs[1], p, x, mod, ng[1], w_out)
        if not shared_mod:
            return _mlp(x, mod, ng[2], ng[3], w1, w2, layer)
        nb, nl, nd = x.shape
        return _mlp(x.reshape(1, nb * nl, nd), mod[:1], ng[2], ng[3], w1, w2, layer).reshape(nb, nl, nd)

    xl = finish(plat, hs_l, xl, ml, False)
    if need_ctx:
        xc = finish(pc, hs_c, xc, mc, True)
    return xc, xl


def _fourier_mla_layer(xc, xl, ml, mc, ng, w1, w2, layer, w_in, w_out, q_g, kv_g, w_uq, w_ukv):
    b, l, d = xl.shape
    n_ctx = xc.shape[1]
    c = N_GROUPS * GROUP
    q_rank = q_g.shape[0]
    kv_rank = kv_g.shape[0]
    kr = w_in[:, c + q_rank + kv_rank:]
    kr1, kr2 = kr[:, :QK_ROPE // 2], kr[:, QK_ROPE // 2:]
    n_used = c + q_rank + kv_rank + 2 * QK_ROPE
    n_pad = -n_used % 256
    w_in_p = jnp.concatenate([w_in[:, :c + q_rank + kv_rank], kr1, kr2, kr2, kr1,
                              jnp.zeros((d, n_pad), w_in.dtype)], axis=1).astype(BF16)
    lk = l + n_ctx
    p = _norm_matmul(xl, ml, ng[0], w_in_p, 0, 1, out_rows=lk)
    tmc = _pick(n_ctx, (512, 256, 128))
    p = _norm_matmul(xc, mc, ng[0], w_in_p, 0, 1, out_rows=lk, row_block_off=l // tmc, into=p)
    cs = _rope_tables(l, n_ctx)
    four = _fourier_mix(p, l)
    q = _mla_queries(p, l, c // q_rank, q_g, w_uq, cs)
    k, v = _mla_keys_values(p, (c + q_rank) // kv_rank, (c + q_rank + kv_rank) // 128, kv_g, w_ukv, cs)
    att = _attention(q, k, v)
    xl = _mixout_cd(four, att, xl, ml, ng[1], w_out.astype(BF16))
    return _mlp(xl, ml, ng[2], ng[3], w1, w2, layer)


def kernel(x, c, ctx, c_ctx, mod_w, mod_b, norm_g, mlp_w1, mlp_w2, ab_w_in, ab_w_out, cv_w, cv_b, cv_norm_g, cv_norm_b, rg_conv_w, rg_conv_b, rg_wa, rg_ba, rg_wi, rg_bi, rg_lambda, cd_w_in, cd_w_out, mla_q_norm_g, mla_kv_norm_g, mla_w_uq, mla_w_ukv):
    b, l, d = x.shape
    depth = mod_w.shape[0]
    assert b + 1 <= 8 and l % GRID_W == 0
    cond8 = jnp.concatenate([c, c_ctx[None, :], jnp.zeros((8 - b - 1, d), F32)], axis=0)
    mods = _modulation(cond8, mod_w, mod_b)
    w1, w2 = mlp_w1.astype(BF16), mlp_w2.astype(BF16)
    xc, xl = ctx, x
    for i in range(depth):
        need_ctx = i < depth - 1
        ml = mods[i, :b].reshape(b, 6, d)
        mc = jnp.broadcast_to(mods[i, b].reshape(1, 6, d), (b, 6, d))
        j = i // 2
        if i % 2 == 0:
            xc, xl = _conv_lru_layer(xc, xl, ml, mc, norm_g[i], w1, w2, i, ab_w_in[j], ab_w_out[j],
                                     cv_w[j], cv_b[j], cv_norm_g[j], cv_norm_b[j], rg_conv_w[j], rg_conv_b[j],
                                     rg_wa[j], rg_ba[j], rg_wi[j], rg_bi[j], rg_lambda[j], need_ctx)
        else:
            xl = _fourier_mla_layer(xc, xl, ml, mc, norm_g[i], w1, w2, i, cd_w_in[j], cd_w_out[j],
                                    mla_q_norm_g[j], mla_kv_norm_g[j], mla_w_uq[j], mla_w_ukv[j])
    return xl
```

```python
import functools
import math

import numpy as np
import jax
import jax.numpy as jnp
from jax import lax
from jax.experimental import pallas as pl
from jax.experimental.pallas import tpu as pltpu

F32 = jnp.float32
BF16 = jnp.bfloat16

EPS = 1e-6
GRID_W = 64
N_GROUPS = 8
GROUP = 128
A_CONV = 31
RG_CONV = 4
RG_C = 8.0
QK_NOPE = 128
QK_ROPE = 64
QK_DIM = QK_NOPE + QK_ROPE
V_DIM = 128
ROPE_BASE = 10000.0
QK_PAD = 256
CONV_HALO = 16
LRU_HALO = 8
FFT_GROUP = 16
VMEM_LIMIT = 56 * 1024 * 1024


def _params(sem, vmem=VMEM_LIMIT):
    return pltpu.CompilerParams(dimension_semantics=sem, vmem_limit_bytes=vmem)


def _pick(n, candidates):
    for c in candidates:
        if n % c == 0:
            return c
    raise ValueError(f"no tile for {n} in {candidates}")


def _rms(x, g):
    return x * lax.rsqrt(jnp.mean(x * x, axis=-1, keepdims=True) + EPS) * g


def _mod_kernel(c_ref, w_ref, b_ref, o_ref):
    c = c_ref[...]
    s = c * jax.nn.sigmoid(c)
    o_ref[0] = jnp.dot(s, w_ref[0], preferred_element_type=F32,
                       precision=lax.Precision.HIGHEST) + b_ref[0]


def _modulation(cond8, mod_w, mod_b):
    depth, d, n = mod_w.shape
    tn = _pick(n, (2048, 1024, 512, 256, 128))
    return pl.pallas_call(
        _mod_kernel,
        grid=(depth, n // tn),
        in_specs=[pl.BlockSpec((8, d), lambda l, j: (0, 0)),
                  pl.BlockSpec((1, d, tn), lambda l, j: (l, 0, j)),
                  pl.BlockSpec((1, 1, tn), lambda l, j: (l, 0, j))],
        out_specs=pl.BlockSpec((1, 8, tn), lambda l, j: (l, 0, j)),
        out_shape=jax.ShapeDtypeStruct((depth, 8, n), F32),
        name="modulation",
        compiler_params=_params(("arbitrary", "arbitrary")),
    )(cond8, mod_w, mod_b.reshape(depth, 1, n))


def _norm_matmul_kernel(x_ref, m_ref, g_ref, w_ref, *rest, shift_row, scale_row):
    o_ref = rest[-1]
    gain = g_ref[...] * (1.0 + m_ref[0, scale_row:scale_row + 1, :])
    x = x_ref[0]
    h = x * lax.rsqrt(jnp.mean(x * x, axis=-1, keepdims=True) + EPS) * gain + m_ref[0, shift_row:shift_row + 1, :]
    o_ref[0] = jnp.dot(h.astype(BF16), w_ref[...], preferred_element_type=F32).astype(o_ref.dtype)


def _norm_matmul(x, mod, g, w, shift_row, scale_row, out_rows=None, row_block_off=0, into=None):
    b, l, d = x.shape
    n = w.shape[1]
    fits = lambda t: 2 * t * d * 4 + 2 * t * n * 4 + d * n * 2 <= VMEM_LIMIT - (8 << 20)
    tm = _pick(l, [t for t in (1024, 512, 256, 128) if fits(t)])
    out_rows = l if out_rows is None else out_rows
    off = row_block_off
    in_specs = [pl.BlockSpec((1, tm, d), lambda bi, i: (bi, i, 0)),
                pl.BlockSpec((1, 6, d), lambda bi, i: (bi, 0, 0)),
                pl.BlockSpec((1, d), lambda bi, i: (0, 0)),
                pl.BlockSpec((d, n), lambda bi, i: (0, 0), pipeline_mode=pl.Buffered(1))]
    args = [x, mod, g.reshape(1, d), w]
    aliases = {}
    if into is not None:
        in_specs.append(pl.BlockSpec(memory_space=pl.ANY))
        args.append(into)
        aliases = {4: 0}
    return pl.pallas_call(
        functools.partial(_norm_matmul_kernel, shift_row=shift_row, scale_row=scale_row),
        grid=(b, l // tm),
        in_specs=in_specs,
        out_specs=pl.BlockSpec((1, tm, n), lambda bi, i: (bi, i + off, 0)),
        out_shape=jax.ShapeDtypeStruct((b, out_rows, n), F32),
        input_output_aliases=aliases,
        name="norm_matmul",
        compiler_params=_params(("arbitrary", "arbitrary")),
    )(*args)


def _conv_kernel(v_ref, gt_ref, vp_ref, gp_ref, vn_ref, gn_ref, w_ref, b_ref, lg_ref, lb_ref,
                 o_ref, ubuf, sh_ref, *, t, rc):
    i = pl.program_id(1)
    first = i == 0
    last = i == pl.num_programs(1) - 1
    halo = CONV_HALO
    ubuf[halo:halo + t, :] = v_ref[0] * jax.nn.sigmoid(gt_ref[0])
    up = vp_ref[0] * jax.nn.sigmoid(gp_ref[0])
    un = vn_ref[0] * jax.nn.sigmoid(gn_ref[0])
    ubuf[0:halo, :] = jnp.where(first, 0.0, up)
    ubuf[halo + t:2 * halo + t, :] = jnp.where(last, 0.0, un)
    base = halo - A_CONV // 2
    span = t + 8 * ((base + A_CONV - 1) // 8)
    for s in range(1, 8):
        sh_ref[s - 1] = ubuf[s:s + span, :]
    for r0 in range(0, t, rc):
        for g in range(N_GROUPS):
            cs = slice(g * GROUP, (g + 1) * GROUP)
            acc = jnp.zeros((rc, GROUP), F32)
            for k in range(A_CONV):
                s, a8 = (base + k) % 8, 8 * ((base + k) // 8)
                src = ubuf if s == 0 else sh_ref.at[s - 1]
                acc = acc + w_ref[k:k + 1, cs] * src[r0 + a8:r0 + a8 + rc, cs]
            y = acc + b_ref[:, cs]
            mu = jnp.mean(y, axis=-1, keepdims=True)
            dlt = y - mu
            yn = dlt * lax.rsqrt(jnp.mean(dlt * dlt, axis=-1, keepdims=True) + EPS)
            z = yn * lg_ref[:, cs] + lb_ref[:, cs]
            o_ref[0, r0:r0 + rc, cs] = (z * jax.nn.sigmoid(z)).astype(o_ref.dtype)


def _conformer_conv(p, cv_w, cv_b, ln_g, ln_b):
    b, l, _ = p.shape
    c = cv_w.shape[1]
    t = _pick(l, (512, 256, 128))
    rc = 64
    hb = t // CONV_HALO
    nhb = l // CONV_HALO
    main = lambda col: pl.BlockSpec((1, t, c), lambda bi, i: (bi, i, col))
    prev = lambda col: pl.BlockSpec((1, CONV_HALO, c), lambda bi, i: (bi, jnp.maximum(i * hb - 1, 0), col))
    nxt = lambda col: pl.BlockSpec((1, CONV_HALO, c), lambda bi, i: (bi, jnp.minimum((i + 1) * hb, nhb - 1), col))
    vec = pl.BlockSpec((1, c), lambda bi, i: (0, 0))
    return pl.pallas_call(
        functools.partial(_conv_kernel, t=t, rc=rc),
        grid=(b, l // t),
        in_specs=[main(0), main(1), prev(0), prev(1), nxt(0), nxt(1),
                  pl.BlockSpec((A_CONV, c), lambda bi, i: (0, 0)), vec, vec, vec],
        out_specs=pl.BlockSpec((1, t, c), lambda bi, i: (bi, i, 0)),
        out_shape=jax.ShapeDtypeStruct((b, l, c), BF16),
        scratch_shapes=[pltpu.VMEM((t + 2 * CONV_HALO, c), F32),
                        pltpu.VMEM((7, t + 8 * ((CONV_HALO + A_CONV // 2) // 8), c), F32)],
        name="conformer_conv",
        compiler_params=_params(("arbitrary", "arbitrary")),
    )(p, p, p, p, p, p, cv_w, cv_b.reshape(1, c), ln_g.reshape(1, c), ln_b.reshape(1, c))


def _lru_kernel(x_ref, xh_ref, cw_ref, cb_ref, wg_ref, ba_ref, bi_ref, lam_ref, h0_ref,
                h_ref, hl_ref, xbuf, a_buf, b_buf, carry, *, t, reverse):
    i = pl.program_id(1)
    c = x_ref.shape[-1]
    halo = LRU_HALO

    @pl.when(i == 0)
    def _():
        carry[...] = h0_ref[0]

    edge = jnp.where(i == 0, 0.0, xh_ref[0])
    if reverse:
        xbuf[0:t, :] = x_ref[0]
        xbuf[t:t + halo, :] = edge
        base = 0
    else:
        xbuf[0:halo, :] = edge
        xbuf[halo:halo + t, :] = x_ref[0]
        base = halo - (RG_CONV - 1)
    xall = xbuf[...]
    n = t + halo
    y = jnp.zeros((t, c), F32) + cb_ref[...]
    for k in range(RG_CONV):
        win = xall if base + k == 0 else pltpu.roll(xall, n - (base + k), 0)
        y = y + cw_ref[k:k + 1, :] * win[:t]

    log_sig = jax.nn.log_sigmoid(lam_ref[...])
    for h in range(N_GROUPS):
        cs = slice(h * GROUP, (h + 1) * GROUP)
        yh = y[:, cs]
        gates = jnp.dot(yh.astype(BF16), wg_ref[h], preferred_element_type=F32)
        r = jax.nn.sigmoid(gates[:, :GROUP] + ba_ref[:, cs])
        ig = jax.nn.sigmoid(gates[:, GROUP:] + bi_ref[:, cs])
        log_a = RG_C * r * log_sig[:, cs]
        a = jnp.exp(log_a)
        a_buf[:, cs] = a
        b_buf[:, cs] = jnp.sqrt(1.0 - a * a) * (ig * yh)

    ngroups = t // 8
    row = lax.broadcasted_iota(jnp.int32, (8, c), 0)

    def body(gidx, cr):
        gi = (ngroups - 1 - gidx) if reverse else gidx
        r0 = pl.multiple_of(gi * 8, 8)
        a = a_buf[pl.ds(r0, 8), :]
        bb = b_buf[pl.ds(r0, 8), :]
        for k in (1, 2, 4):
            if reverse:
                a_s = pltpu.roll(a, 8 - k, 0)
                b_s = pltpu.roll(bb, 8 - k, 0)
                valid = row < 8 - k
            else:
                a_s = pltpu.roll(a, k, 0)
                b_s = pltpu.roll(bb, k, 0)
                valid = row >= k
            bb = jnp.where(valid, a * b_s + bb, bb)
            a = jnp.where(valid, a * a_s, a)
        hh = bb + a * cr
        h_ref[0, pl.ds(r0, 8), :] = hh
        edge_row = hh[0:1, :] if reverse else hh[7:8, :]
        return jnp.broadcast_to(edge_row, (8, c))

    cr = lax.fori_loop(0, ngroups, body, carry[...], unroll=4)
    carry[...] = cr
    hl_ref[0] = cr


def _rglru(p, col, h0, conv_w, conv_b, wa, ba, wi, bi, lam, reverse):
    b, l, _ = p.shape
    c = conv_w.shape[1]
    t = _pick(l, (1024, 512, 256, 128))
    nt = l // t
    hb = t // LRU_HALO
    nhb = l // LRU_HALO
    if reverse:
        tile = lambda bi_, i: (bi_, nt - 1 - i, col)
        halo = lambda bi_, i: (bi_, jnp.minimum((nt - i) * hb, nhb - 1), col)
        otile = lambda bi_, i: (bi_, nt - 1 - i, 0)
    else:
        tile = lambda bi_, i: (bi_, i, col)
        halo = lambda bi_, i: (bi_, jnp.maximum(i * hb - 1, 0), col)
        otile = lambda bi_, i: (bi_, i, 0)
    wg = jnp.concatenate([wa, wi], axis=-1).astype(BF16)
    vec = pl.BlockSpec((1, c), lambda bi_, i: (0, 0))
    state = pl.BlockSpec((1, 8, c), lambda bi_, i: (bi_, 0, 0))
    return pl.pallas_call(
        functools.partial(_lru_kernel, t=t, reverse=reverse),
        grid=(b, nt),
        in_specs=[pl.BlockSpec((1, t, c), tile),
                  pl.BlockSpec((1, LRU_HALO, c), halo),
                  pl.BlockSpec((RG_CONV, c), lambda bi_, i: (0, 0)), vec,
                  pl.BlockSpec((N_GROUPS, GROUP, 2 * GROUP), lambda bi_, i: (0, 0, 0)),
                  vec, vec, vec, state],
        out_specs=[pl.BlockSpec((1, t, c), otile), state],
        out_shape=[jax.ShapeDtypeStruct((b, l, c), F32), jax.ShapeDtypeStruct((b, 8, c), F32)],
        scratch_shapes=[pltpu.VMEM((t + LRU_HALO, c), F32), pltpu.VMEM((t, c), F32),
                        pltpu.VMEM((t, c), F32), pltpu.VMEM((8, c), F32)],
        name="rglru_rev" if reverse else "rglru_fwd",
        compiler_params=_params(("arbitrary", "arbitrary")),
    )(p, p, conv_w, conv_b.reshape(1, c), wg, ba.reshape(1, c), bi.reshape(1, c), lam.reshape(1, c), h0)


def _mixout_ab_kernel(cv_ref, hf_ref, hr_ref, pg_ref, x_ref, m_ref, g_ref, wa_ref, wb_ref, o_ref):
    rec = (hf_ref[0] + hr_ref[0]) * jax.nn.gelu(pg_ref[0])
    y = jnp.dot(cv_ref[0], wa_ref[...], preferred_element_type=F32)
    y = y + jnp.dot(rec.astype(BF16), wb_ref[...], preferred_element_type=F32)
    o_ref[0] = x_ref[0] + m_ref[0, 2:3, :] * _rms(y, g_ref[...])


def _mixout_ab(conv_out, h_f, h_r, p, x, mod, g, w_out):
    b, l, d = x.shape
    c = conv_out.shape[-1]
    tm = _pick(l, (512, 256, 128))
    half = lambda: pl.BlockSpec((1, tm, c), lambda bi, i: (bi, i, 0))
    wspec = lambda k: pl.BlockSpec((c, d), lambda bi, i: (k, 0))
    return pl.pallas_call(
        _mixout_ab_kernel,
        grid=(b, l // tm),
        in_specs=[half(), half(), half(),
                  pl.BlockSpec((1, tm, c), lambda bi, i: (bi, i, 3)),
                  pl.BlockSpec((1, tm, d), lambda bi, i: (bi, i, 0)),
                  pl.BlockSpec((1, 6, d), lambda bi, i: (bi, 0, 0)),
                  pl.BlockSpec((1, d), lambda bi, i: (0, 0)),
                  wspec(0), wspec(1)],
        out_specs=pl.BlockSpec((1, tm, d), lambda bi, i: (bi, i, 0)),
        out_shape=jax.ShapeDtypeStruct((b, l, d), F32),
        name="mixout_ab",
        compiler_params=_params(("arbitrary", "arbitrary")),
    )(conv_out, h_f, h_r, p, x, mod, g.reshape(1, d), w_out, w_out)


def _mixout_cd_kernel(f_ref, o_att_ref, x_ref, m_ref, g_ref, wa_ref, wb_ref, o_ref):
    y = jnp.dot(f_ref[0].astype(BF16), wa_ref[...], preferred_element_type=F32)
    y = y + jnp.dot(o_att_ref[0], wb_ref[...], preferred_element_type=F32)
    o_ref[0] = x_ref[0] + m_ref[0, 2:3, :] * _rms(y, g_ref[...])


def _mixout_cd(four, att, x, mod, g, w_out):
    b, l, d = x.shape
    c = four.shape[-1]
    tm = _pick(l, (512, 256, 128))
    half = lambda: pl.BlockSpec((1, tm, c), lambda bi, i: (bi, i, 0))
    wspec = lambda k: pl.BlockSpec((c, d), lambda bi, i: (k, 0))
    return pl.pallas_call(
        _mixout_cd_kernel,
        grid=(b, l // tm),
        in_specs=[half(), half(),
                  pl.BlockSpec((1, tm, d), lambda bi, i: (bi, i, 0)),
                  pl.BlockSpec((1, 6, d), lambda bi, i: (bi, 0, 0)),
                  pl.BlockSpec((1, d), lambda bi, i: (0, 0)),
                  wspec(0), wspec(1)],
        out_specs=pl.BlockSpec((1, tm, d), lambda bi, i: (bi, i, 0)),
        out_shape=jax.ShapeDtypeStruct((b, l, d), F32),
        name="mixout_cd",
        compiler_params=_params(("arbitrary", "arbitrary")),
    )(four, att, x, mod, g.reshape(1, d), w_out, w_out)


def _mlp_kernel(x_ref, m_ref, gpre_ref, gpost_ref, w1_ref, w2_ref, o_ref, h_ref, acc_ref):
    j = pl.program_id(2)
    last = pl.num_programs(2) - 1

    def chunk(h):
        u = jnp.dot(h, w1_ref[...], preferred_element_type=F32)
        u = jnp.square(jnp.maximum(u, 0.0))
        return jnp.dot(u.astype(BF16), w2_ref[...], preferred_element_type=F32)

    @pl.when(j == 0)
    def _():
        gain = gpre_ref[...] * (1.0 + m_ref[0, 4:5, :])
        x = x_ref[0]
        h = x * lax.rsqrt(jnp.mean(x * x, axis=-1, keepdims=True) + EPS) * gain + m_ref[0, 3:4, :]
        h = h.astype(BF16)
        h_ref[...] = h
        acc_ref[...] = chunk(h)

    @pl.when((j > 0) & (j < last))
    def _():
        acc_ref[...] += chunk(h_ref[...])

    @pl.when(j == last)
    def _():
        y = acc_ref[...] + chunk(h_ref[...])
        gain = m_ref[0, 5:6, :] * gpost_ref[...]
        o_ref[0] = x_ref[0] + y * lax.rsqrt(jnp.mean(y * y, axis=-1, keepdims=True) + EPS) * gain


def _mlp(x, mod, g_pre, g_post, w1, w2, layer):
    b, l, d = x.shape
    f = w1.shape[2]
    tm = _pick(l, (512, 256, 128))
    tf = _pick(f, (1024, 512, 256, 128))
    assert f // tf >= 2
    return pl.pallas_call(
        _mlp_kernel,
        grid=(b, l // tm, f // tf),
        in_specs=[pl.BlockSpec((1, tm, d), lambda bi, i, j: (bi, i, 0)),
                  pl.BlockSpec((1, 6, d), lambda bi, i, j: (bi, 0, 0)),
                  pl.BlockSpec((1, d), lambda bi, i, j: (0, 0)),
                  pl.BlockSpec((1, d), lambda bi, i, j: (0, 0)),
                  pl.BlockSpec((None, d, tf), lambda bi, i, j: (layer, 0, j)),
                  pl.BlockSpec((None, tf, d), lambda bi, i, j: (layer, j, 0))],
        out_specs=pl.BlockSpec((1, tm, d), lambda bi, i, j: (bi, i, 0)),
        out_shape=jax.ShapeDtypeStruct((b, l, d), F32),
        scratch_shapes=[pltpu.VMEM((tm, d), BF16), pltpu.VMEM((tm, d), F32)],
        name="mlp",
        compiler_params=_params(("arbitrary", "arbitrary", "arbitrary")),
    )(x, mod, g_pre.reshape(1, d), g_post.reshape(1, d), w1, w2)


def _fourier_tables(l):
    n1 = 1 << (int(math.log2(l)) // 2)
    while l % n1:
        n1 //= 2
    n2 = l // n1
    def cs(n):
        idx = np.arange(n)
        ang = 2.0 * np.pi * ((idx[:, None] * idx[None, :]) % n) / n
        return np.cos(ang), np.sin(ang)
    c1, s1 = cs(n1)
    c2, s2 = cs(n2)
    cc, sc = cs(GROUP)
    delta = 2.0 * np.pi * np.arange(n1)[:, None] / l * np.ones((1, GROUP))
    norm = 1.0 / math.sqrt(l * GROUP)
    chan = np.concatenate([cc, -sc], axis=1)
    f1 = np.concatenate([c1, s1], axis=0)
    f2 = np.concatenate([c2, s2], axis=1) * norm
    return (n1, n2, jnp.asarray(chan, BF16), jnp.asarray(f1, BF16), jnp.asarray(f2, BF16),
            jnp.asarray(np.cos(delta), F32), jnp.asarray(np.sin(delta), F32))


def _fourier_kernel(x_ref, chan_ref, f1_ref, f2_ref, dc_ref, ds_ref, o_ref, are_buf, aim_buf, tw_c, tw_s,
                    *, n1, n2):
    tw_c[...] = jnp.ones_like(tw_c)
    tw_s[...] = jnp.zeros_like(tw_s)

    def stage1(jg, _):
        j0 = jg * FFT_GROUP
        xs = jnp.concatenate([x_ref[0, pl.ds(j0 + u, n1, stride=n2), :] for u in range(FFT_GROUP)], axis=0)
        pq = jnp.dot(xs.astype(BF16), chan_ref[...], preferred_element_type=F32)
        pq = jnp.concatenate([pq[u * n1:(u + 1) * n1] for u in range(FFT_GROUP)], axis=1).astype(BF16)
        r = jnp.dot(f1_ref[...], pq, preferred_element_type=F32)
        for u in range(FFT_GROUP):
            ru = r[:, u * 2 * GROUP:(u + 1) * 2 * GROUP]
            a_re = ru[:n1, :GROUP] + ru[n1:, GROUP:]
            a_im = ru[:n1, GROUP:] - ru[n1:, :GROUP]
            tc = tw_c[...]
            ts = tw_s[...]
            row0 = pl.multiple_of((j0 + u) * n1, 8)
            are_buf[pl.ds(row0, n1), :] = a_re * tc + a_im * ts
            aim_buf[pl.ds(row0, n1), :] = a_im * tc - a_re * ts
            tw_c[...] = tc * dc_ref[...] - ts * ds_ref[...]
            tw_s[...] = ts * dc_ref[...] + tc * ds_ref[...]
        return 0

    lax.fori_loop(0, n2 // FFT_GROUP, stage1, 0)

    def stage2(kg, _):
        k0 = kg * FFT_GROUP
        rhs = jnp.concatenate(
            [jnp.concatenate([are_buf[pl.ds(k0 + u, n2, stride=n1), :], aim_buf[pl.ds(k0 + u, n2, stride=n1), :]],
                             axis=0) for u in range(FFT_GROUP)], axis=1).astype(BF16)
        y = jnp.dot(f2_ref[...], rhs, preferred_element_type=F32)
        for u in range(FFT_GROUP):
            o_ref[0, pl.ds(k0 + u, n2, stride=n1), :] = y[:, u * GROUP:(u + 1) * GROUP]
        return 0

    lax.fori_loop(0, n1 // FFT_GROUP, stage2, 0)


def _fourier_mix(p, l):
    b = p.shape[0]
    n1, n2, chan, f1, f2, dc, ds = _fourier_tables(l)
    const = lambda a: pl.BlockSpec(a.shape, lambda bi, g: (0, 0))
    return pl.pallas_call(
        functools.partial(_fourier_kernel, n1=n1, n2=n2),
        grid=(b, N_GROUPS),
        in_specs=[pl.BlockSpec((1, l, GROUP), lambda bi, g: (bi, 0, g)),
                  const(chan), const(f1), const(f2), const(dc), const(ds)],
        out_specs=pl.BlockSpec((1, l, GROUP), lambda bi, g: (bi, 0, g)),
        out_shape=jax.ShapeDtypeStruct((b, l, N_GROUPS * GROUP), F32),
        scratch_shapes=[pltpu.VMEM((l, GROUP), F32), pltpu.VMEM((l, GROUP), F32),
                        pltpu.VMEM((n1, GROUP), F32), pltpu.VMEM((n1, GROUP), F32)],
        name="fourier_mix",
        compiler_params=_params(("arbitrary", "arbitrary"), 62 * 1024 * 1024),
    )(p, chan, f1, f2, dc, ds)


def _rope_tables(l, n_ctx):
    t = np.arange(l)
    inv = ROPE_BASE ** (-np.arange(QK_ROPE // 4, dtype=np.float64) / (QK_ROPE // 4))
    ang = np.concatenate([(t // GRID_W)[:, None] * inv, (t % GRID_W)[:, None] * inv], axis=-1)
    ang = np.concatenate([ang, np.zeros((n_ctx, QK_ROPE // 2))], axis=0)
    cos, sin = np.cos(ang), np.sin(ang)
    return jnp.asarray(np.concatenate([cos, cos, -sin, sin], axis=-1), F32)


def _rope_apply(r, cs):
    tt = r * cs
    ro = tt + pltpu.roll(tt, QK_ROPE, 1)
    lane = lax.broadcasted_iota(jnp.int32, ro.shape, 1)
    return jnp.where(lane < QK_ROPE, ro, 0.0)


def _q_kernel(cq_ref, g_ref, w_ref, cs_ref, o_ref):
    n = _rms(cq_ref[0], g_ref[...]).astype(BF16)
    cs = cs_ref[...]
    for h in range(N_GROUPS):
        res = jnp.dot(n, w_ref[h], preferred_element_type=F32)
        q = jnp.concatenate([res[:, :QK_NOPE], _rope_apply(res[:, QK_NOPE:], cs)], axis=1)
        q = q * (math.log2(math.e) * QK_DIM ** -0.5)
        o_ref[0, h] = q.T.astype(o_ref.dtype)


def _mla_queries(p, l, col, q_g, w_uq, cs):
    b = p.shape[0]
    rank = w_uq.shape[0]
    w = w_uq.reshape(rank, N_GROUPS, QK_DIM)
    x1 = w[:, :, QK_NOPE:QK_NOPE + QK_ROPE // 2]
    x2 = w[:, :, QK_NOPE + QK_ROPE // 2:]
    w = jnp.concatenate([w[:, :, :QK_NOPE], x1, x2, x2, x1], axis=-1)
    w = jnp.transpose(w, (1, 0, 2)).astype(BF16)
    tm = _pick(l, (1024, 512, 256, 128))
    return pl.pallas_call(
        _q_kernel,
        grid=(b, l // tm),
        in_specs=[pl.BlockSpec((1, tm, rank), lambda bi, i: (bi, i, col)),
                  pl.BlockSpec((1, rank), lambda bi, i: (0, 0)),
                  pl.BlockSpec((N_GROUPS, rank, QK_PAD), lambda bi, i: (0, 0, 0)),
                  pl.BlockSpec((tm, 128), lambda bi, i: (i, 0))],
        out_specs=pl.BlockSpec((1, N_GROUPS, QK_PAD, tm), lambda bi, i: (bi, 0, 0, i)),
        out_shape=jax.ShapeDtypeStruct((b, N_GROUPS, QK_PAD, l), BF16),
        name="mla_q",
        compiler_params=_params(("arbitrary", "arbitrary")),
    )(p, q_g.reshape(1, rank), w, cs)


def _kv_kernel(ckv_ref, kr_ref, g_ref, w_ref, cs_ref, k_ref, v_ref):
    n = _rms(ckv_ref[0], g_ref[...]).astype(BF16)
    rope = _rope_apply(kr_ref[0], cs_ref[...]).astype(BF16)
    hw = QK_NOPE + V_DIM
    for h in range(N_GROUPS):
        res = jnp.dot(n, w_ref[:, h * hw:(h + 1) * hw], preferred_element_type=F32)
        k_ref[0, h] = jnp.concatenate([res[:, :QK_NOPE].astype(BF16), rope], axis=1)
        vt = res[:, QK_NOPE:].T.astype(BF16)
        for c in range(vt.shape[1] // 128):
            v_ref[0, h, c] = vt[:, c * 128:(c + 1) * 128]


def _mla_keys_values(p, col_kv, col_kr, kv_g, w_ukv, cs):
    b, lk, _ = p.shape
    rank = w_ukv.shape[0]
    tm = _pick(lk, (1280, 640, 256, 128))
    return pl.pallas_call(
        _kv_kernel,
        grid=(b, lk // tm),
        in_specs=[pl.BlockSpec((1, tm, rank), lambda bi, i: (bi, i, col_kv)),
                  pl.BlockSpec((1, tm, 128), lambda bi, i: (bi, i, col_kr)),
                  pl.BlockSpec((1, rank), lambda bi, i: (0, 0)),
                  pl.BlockSpec(w_ukv.shape, lambda bi, i: (0, 0)),
                  pl.BlockSpec((tm, 128), lambda bi, i: (i, 0))],
        out_specs=[pl.BlockSpec((1, N_GROUPS, tm, QK_PAD), lambda bi, i: (bi, 0, i, 0)),
                   pl.BlockSpec((1, N_GROUPS, tm // 128, V_DIM, 128), lambda bi, i: (bi, 0, i, 0, 0))],
        out_shape=[jax.ShapeDtypeStruct((b, N_GROUPS, lk, QK_PAD), BF16),
                   jax.ShapeDtypeStruct((b, N_GROUPS, lk // 128, V_DIM, 128), BF16)],
        name="mla_kv",
        compiler_params=_params(("arbitrary", "arbitrary")),
    )(p, p, kv_g.reshape(1, rank), w_ukv.astype(BF16), cs)


def _attn_kernel(qt_ref, k_ref, vt_ref, o_ref, m_ref, l_ref, acc_ref, sa_ref, sb_ref, *, tks, n_sub, rem):
    qt = qt_ref[0, 0]

    def scores(r0, width):
        return jnp.dot(k_ref[0, 0, pl.ds(r0, width), :], qt, preferred_element_type=F32)

    def update(s, c0, width):
        m_prev = m_ref[...]
        m_next = jnp.maximum(m_prev, jnp.max(s, axis=0, keepdims=True))
        p = jnp.exp2(s - pltpu.repeat(m_next, width // 8, axis=0))
        alpha = jnp.exp2(m_prev - m_next)
        l_ref[...] = alpha * l_ref[...] + jnp.sum(p, axis=0, keepdims=True)
        vt = jnp.concatenate([vt_ref[0, 0, c0 + c] for c in range(width // 128)], axis=1)
        pv = jnp.dot(vt, p.astype(BF16), preferred_element_type=F32)
        acc_ref[...] = pltpu.repeat(alpha, V_DIM // 8, axis=0) * acc_ref[...] + pv
        m_ref[...] = m_next

    def start(j):
        return pl.multiple_of(j * tks, tks)

    cpt = tks // 128
    m_ref[...] = jnp.full_like(m_ref, -jnp.inf)
    l_ref[...] = jnp.zeros_like(l_ref)
    acc_ref[...] = jnp.zeros_like(acc_ref)
    sa_ref[...] = scores(0, tks)

    def pair(i, _):
        sb_ref[...] = scores(start(2 * i + 1), tks)
        update(sa_ref[...], 2 * i * cpt, tks)
        sa_ref[...] = scores(start(2 * i + 2), tks)
        update(sb_ref[...], (2 * i + 1) * cpt, tks)
        return 0

    n_pairs = n_sub // 2
    lax.fori_loop(0, n_pairs - 1, pair, 0, unroll=3 if (n_pairs - 1) % 3 == 0 else 1)
    done = 2 * (n_pairs - 1)
    sb_ref[...] = scores(done * tks + tks, tks)
    update(sa_ref[...], done * cpt, tks)
    tail = [(j * tks, tks) for j in range(done + 2, n_sub)] + ([(n_sub * tks, rem)] if rem else [])
    bufs = [sa_ref, sb_ref]
    pending = (sb_ref, (done + 1) * tks, tks)
    for idx, (r0, width) in enumerate(tail):
        nxt = bufs[idx % 2]
        nxt[:width, :] = scores(r0, width)
        update(pending[0][:pending[2], :], pending[1] // 128, pending[2])
        pending = (nxt, r0, width)
    update(pending[0][:pending[2], :], pending[1] // 128, pending[2])
    out_t = acc_ref[...] / pltpu.repeat(l_ref[...], V_DIM // 8, axis=0)
    o_ref[0] = out_t.T.astype(o_ref.dtype)


def _attention(qt, k, vt):
    b, h, _, l = qt.shape
    lk = k.shape[2]
    tq = _pick(l, (1024, 512, 256, 128))
    tks = 512
    n_sub, rem = lk // tks, lk % tks
    assert n_sub >= 2 and rem % 128 == 0
    return pl.pallas_call(
        functools.partial(_attn_kernel, tks=tks, n_sub=n_sub, rem=rem),
        grid=(b, h, l // tq),
        in_specs=[pl.BlockSpec((1, 1, QK_PAD, tq), lambda bi, hi, i: (bi, hi, 0, i)),
                  pl.BlockSpec((1, 1, lk, QK_PAD), lambda bi, hi, i: (bi, hi, 0, 0)),
                  pl.BlockSpec((1, 1, lk // 128, V_DIM, 128), lambda bi, hi, i: (bi, hi, 0, 0, 0))],
        out_specs=pl.BlockSpec((1, tq, V_DIM), lambda bi, hi, i: (bi, i, hi)),
        out_shape=jax.ShapeDtypeStruct((b, l, h * V_DIM), BF16),
        scratch_shapes=[pltpu.VMEM((8, tq), F32), pltpu.VMEM((8, tq), F32), pltpu.VMEM((V_DIM, tq), F32),
                        pltpu.VMEM((tks, tq), F32), pltpu.VMEM((tks, tq), F32)],
        name="flash_attention",
        compiler_params=_params(("arbitrary", "arbitrary", "arbitrary")),
    )(qt, k, vt)


def _conv_lru_layer(xc, xl, ml, mc, ng, w1, w2, layer, w_in, w_out, cv_w, cv_b, cv_g, cv_beta,
                    rg_conv_w, rg_conv_b, rg_wa, rg_ba, rg_wi, rg_bi, rg_lambda, need_ctx):
    w_in = w_in.astype(BF16)
    w_out = w_out.astype(BF16)
    pc = _norm_matmul(xc, mc, ng[0], w_in, 0, 1)
    plat = _norm_matmul(xl, ml, ng[0], w_in, 0, 1)
    c = cv_w.shape[1]
    zero = jnp.zeros((xl.shape[0], 8, c), F32)
    hs_c, hs_l = [], []
    for d, reverse in enumerate((False, True)):
        args = (rg_conv_w[d], rg_conv_b[d], rg_wa[d], rg_ba[d], rg_wi[d], rg_bi[d], rg_lambda[d], reverse)
        h_c, last_c = _rglru(pc, 2, zero, *args)
        h_l, _ = _rglru(plat, 2, last_c, *args)
        hs_c.append(h_c)
        hs_l.append(h_l)

    def finish(p, hs, x, mod, shared_mod):
        conv_out = _conformer_conv(p, cv_w, cv_b, cv_g, cv_beta)
        x = _mixout_ab(conv_out, hs[0], hs[1], p, x, mod, ng[1], w_out)
        if not shared_mod:
            return _mlp(x, mod, ng[2], ng[3], w1, w2, layer)
        nb, nl, nd = x.shape
        return _mlp(x.reshape(1, nb * nl, nd), mod[:1], ng[2], ng[3], w1, w2, layer).reshape(nb, nl, nd)

    xl = finish(plat, hs_l, xl, ml, False)
    if need_ctx:
        xc = finish(pc, hs_c, xc, mc, True)
    return xc, xl


def _fourier_mla_layer(xc, xl, ml, mc, ng, w1, w2, layer, w_in, w_out, q_g, kv_g, w_uq, w_ukv):
    b, l, d = xl.shape
    n_ctx = xc.shape[1]
    c = N_GROUPS * GROUP
    q_rank = q_g.shape[0]
    kv_rank = kv_g.shape[0]
    kr = w_in[:, c + q_rank + kv_rank:]
    kr1, kr2 = kr[:, :QK_ROPE // 2], kr[:, QK_ROPE // 2:]
    n_used = c + q_rank + kv_rank + 2 * QK_ROPE
    n_pad = -n_used % 256
    w_in_p = jnp.concatenate([w_in[:, :c + q_rank + kv_rank], kr1, kr2, kr2, kr1,
                              jnp.zeros((d, n_pad), w_in.dtype)], axis=1).astype(BF16)
    lk = l + n_ctx
    p = _norm_matmul(xl, ml, ng[0], w_in_p, 0, 1, out_rows=lk)
    tmc = _pick(n_ctx, (512, 256, 128))
    p = _norm_matmul(xc, mc, ng[0], w_in_p, 0, 1, out_rows=lk, row_block_off=l // tmc, into=p)
    cs = _rope_tables(l, n_ctx)
    four = _fourier_mix(p, l)
    q = _mla_queries(p, l, c // q_rank, q_g, w_uq, cs)
    k, v = _mla_keys_values(p, (c + q_rank) // kv_rank, (c + q_rank + kv_rank) // 128, kv_g, w_ukv, cs)
    att = _attention(q, k, v)
    xl = _mixout_cd(four, att, xl, ml, ng[1], w_out.astype(BF16))
    return _mlp(xl, ml, ng[2], ng[3], w1, w2, layer)


def kernel(x, c, ctx, c_ctx, mod_w, mod_b, norm_g, mlp_w1, mlp_w2, ab_w_in, ab_w_out, cv_w, cv_b, cv_norm_g, cv_norm_b, rg_conv_w, rg_conv_b, rg_wa, rg_ba, rg_wi, rg_bi, rg_lambda, cd_w_in, cd_w_out, mla_q_norm_g, mla_kv_norm_g, mla_w_uq, mla_w_ukv):
    b, l, d = x.shape
    depth = mod_w.shape[0]
    assert b + 1 <= 8 and l % GRID_W == 0
    cond8 = jnp.concatenate([c, c_ctx[None, :], jnp.zeros((8 - b - 1, d), F32)], axis=0)
    mods = _modulation(cond8, mod_w, mod_b)
    w1, w2 = mlp_w1.astype(BF16), mlp_w2.astype(BF16)
    xc, xl = ctx, x
    for i in range(depth):
        need_ctx = i < depth - 1
        ml = mods[i, :b].reshape(b, 6, d)
        mc = jnp.broadcast_to(mods[i, b].reshape(1, 6, d), (b, 6, d))
        j = i // 2
        if i % 2 == 0:
            xc, xl = _conv_lru_layer(xc, xl, ml, mc, norm_g[i], w1, w2, i, ab_w_in[j], ab_w_out[j],
                                     cv_w[j], cv_b[j], cv_norm_g[j], cv_norm_b[j], rg_conv_w[j], rg_conv_b[j],
                                     rg_wa[j], rg_ba[j], rg_wi[j], rg_bi[j], rg_lambda[j], need_ctx)
        else:
            xl = _fourier_mla_layer(xc, xl, ml, mc, norm_g[i], w1, w2, i, cd_w_in[j], cd_w_out[j],
                                    mla_q_norm_g[j], mla_kv_norm_g[j], mla_w_uq[j], mla_w_ukv[j])
    return xl
```

```python
import functools
import math

import numpy as np
import jax
import jax.numpy as jnp
from jax import lax
from jax.experimental import pallas as pl
from jax.experimental.pallas import tpu as pltpu

F32 = jnp.float32
BF16 = jnp.bfloat16

EPS = 1e-6
GRID_W = 64
N_GROUPS = 8
GROUP = 128
A_CONV = 31
RG_CONV = 4
RG_C = 8.0
QK_NOPE = 128
QK_ROPE = 64
QK_DIM = QK_NOPE + QK_ROPE
V_DIM = 128
ROPE_BASE = 10000.0
QK_PAD = 256
CONV_HALO = 16
LRU_HALO = 8
FFT_GROUP = 16
FFT_PITCH_PAD = 8
VMEM_LIMIT = 56 * 1024 * 1024


def _params(sem, vmem=VMEM_LIMIT):
    return pltpu.CompilerParams(dimension_semantics=sem, vmem_limit_bytes=vmem)


def _pick(n, candidates):
    for c in candidates:
        if n % c == 0:
            return c
    raise ValueError(f"no tile for {n} in {candidates}")


def _rms(x, g):
    return x * lax.rsqrt(jnp.mean(x * x, axis=-1, keepdims=True) + EPS) * g


def _mod_kernel(c_ref, w_ref, b_ref, o_ref):
    c = c_ref[...]
    s = c * jax.nn.sigmoid(c)
    o_ref[0] = jnp.dot(s, w_ref[0], preferred_element_type=F32,
                       precision=lax.Precision.HIGHEST) + b_ref[0]


def _modulation(cond8, mod_w, mod_b):
    depth, d, n = mod_w.shape
    tn = _pick(n, (2048, 1024, 512, 256, 128))
    return pl.pallas_call(
        _mod_kernel,
        grid=(depth, n // tn),
        in_specs=[pl.BlockSpec((8, d), lambda l, j: (0, 0)),
                  pl.BlockSpec((1, d, tn), lambda l, j: (l, 0, j)),
                  pl.BlockSpec((1, 1, tn), lambda l, j: (l, 0, j))],
        out_specs=pl.BlockSpec((1, 8, tn), lambda l, j: (l, 0, j)),
        out_shape=jax.ShapeDtypeStruct((depth, 8, n), F32),
        name="modulation",
        compiler_params=_params(("arbitrary", "arbitrary")),
    )(cond8, mod_w, mod_b.reshape(depth, 1, n))


def _norm_matmul_kernel(x_ref, m_ref, g_ref, w_ref, *rest, shift_row, scale_row):
    o_ref = rest[-1]
    gain = g_ref[...] * (1.0 + m_ref[0, scale_row:scale_row + 1, :])
    x = x_ref[0]
    h = x * lax.rsqrt(jnp.mean(x * x, axis=-1, keepdims=True) + EPS) * gain + m_ref[0, shift_row:shift_row + 1, :]
    o_ref[0] = jnp.dot(h.astype(BF16), w_ref[...], preferred_element_type=F32).astype(o_ref.dtype)


def _norm_matmul(x, mod, g, w, shift_row, scale_row, out_rows=None, row_block_off=0, into=None):
    b, l, d = x.shape
    n = w.shape[1]
    tm = _pick(l, (512, 256, 128))
    out_rows = l if out_rows is None else out_rows
    off = row_block_off
    in_specs = [pl.BlockSpec((1, tm, d), lambda bi, i: (bi, i, 0)),
                pl.BlockSpec((1, 6, d), lambda bi, i: (bi, 0, 0)),
                pl.BlockSpec((1, d), lambda bi, i: (0, 0)),
                pl.BlockSpec((d, n), lambda bi, i: (0, 0), pipeline_mode=pl.Buffered(1))]
    args = [x, mod, g.reshape(1, d), w]
    aliases = {}
    if into is not None:
        in_specs.append(pl.BlockSpec(memory_space=pl.ANY))
        args.append(into)
        aliases = {4: 0}
    return pl.pallas_call(
        functools.partial(_norm_matmul_kernel, shift_row=shift_row, scale_row=scale_row),
        grid=(b, l // tm),
        in_specs=in_specs,
        out_specs=pl.BlockSpec((1, tm, n), lambda bi, i: (bi, i + off, 0)),
        out_shape=jax.ShapeDtypeStruct((b, out_rows, n), F32),
        input_output_aliases=aliases,
        name="norm_matmul",
        compiler_params=_params(("arbitrary", "arbitrary")),
    )(*args)


def _conv_kernel(v_ref, gt_ref, vp_ref, gp_ref, vn_ref, gn_ref, w_ref, b_ref, lg_ref, lb_ref,
                 o_ref, ubuf, sh_ref, *, t, rc):
    i = pl.program_id(1)
    first = i == 0
    last = i == pl.num_programs(1) - 1
    halo = CONV_HALO
    ubuf[halo:halo + t, :] = v_ref[0] * jax.nn.sigmoid(gt_ref[0])
    up = vp_ref[0] * jax.nn.sigmoid(gp_ref[0])
    un = vn_ref[0] * jax.nn.sigmoid(gn_ref[0])
    ubuf[0:halo, :] = jnp.where(first, 0.0, up)
    ubuf[halo + t:2 * halo + t, :] = jnp.where(last, 0.0, un)
    base = halo - A_CONV // 2
    span = t + 8 * ((base + A_CONV - 1) // 8)
    for s in range(1, 8):
        sh_ref[s - 1] = ubuf[s:s + span, :]
    for r0 in range(0, t, rc):
        for g in range(N_GROUPS):
            cs = slice(g * GROUP, (g + 1) * GROUP)
            acc = jnp.zeros((rc, GROUP), F32)
            for k in range(A_CONV):
                s, a8 = (base + k) % 8, 8 * ((base + k) // 8)
                src = ubuf if s == 0 else sh_ref.at[s - 1]
                acc = acc + w_ref[k:k + 1, cs] * src[r0 + a8:r0 + a8 + rc, cs]
            y = acc + b_ref[:, cs]
            mu = jnp.mean(y, axis=-1, keepdims=True)
            dlt = y - mu
            yn = dlt * lax.rsqrt(jnp.mean(dlt * dlt, axis=-1, keepdims=True) + EPS)
            z = yn * lg_ref[:, cs] + lb_ref[:, cs]
            o_ref[0, r0:r0 + rc, cs] = (z * jax.nn.sigmoid(z)).astype(o_ref.dtype)


def _conformer_conv(p, cv_w, cv_b, ln_g, ln_b):
    b, l, _ = p.shape
    c = cv_w.shape[1]
    t = _pick(l, (512, 256, 128))
    rc = 64
    hb = t // CONV_HALO
    nhb = l // CONV_HALO
    main = lambda col: pl.BlockSpec((1, t, c), lambda bi, i: (bi, i, col))
    prev = lambda col: pl.BlockSpec((1, CONV_HALO, c), lambda bi, i: (bi, jnp.maximum(i * hb - 1, 0), col))
    nxt = lambda col: pl.BlockSpec((1, CONV_HALO, c), lambda bi, i: (bi, jnp.minimum((i + 1) * hb, nhb - 1), col))
    vec = pl.BlockSpec((1, c), lambda bi, i: (0, 0))
    return pl.pallas_call(
        functools.partial(_conv_kernel, t=t, rc=rc),
        grid=(b, l // t),
        in_specs=[main(0), main(1), prev(0), prev(1), nxt(0), nxt(1),
                  pl.BlockSpec((A_CONV, c), lambda bi, i: (0, 0)), vec, vec, vec],
        out_specs=pl.BlockSpec((1, t, c), lambda bi, i: (bi, i, 0)),
        out_shape=jax.ShapeDtypeStruct((b, l, c), BF16),
        scratch_shapes=[pltpu.VMEM((t + 2 * CONV_HALO, c), F32),
                        pltpu.VMEM((7, t + 8 * ((CONV_HALO + A_CONV // 2) // 8), c), F32)],
        name="conformer_conv",
        compiler_params=_params(("arbitrary", "arbitrary")),
    )(p, p, p, p, p, p, cv_w, cv_b.reshape(1, c), ln_g.reshape(1, c), ln_b.reshape(1, c))


def _lru_kernel(x_ref, xh_ref, cw_ref, cb_ref, wg_ref, ba_ref, bi_ref, lam_ref, h0_ref,
                h_ref, hl_ref, xbuf, a_buf, b_buf, carry, *, t, reverse):
    i = pl.program_id(1)
    c = x_ref.shape[-1]
    halo = LRU_HALO

    @pl.when(i == 0)
    def _():
        carry[...] = h0_ref[0]

    edge = jnp.where(i == 0, 0.0, xh_ref[0])
    if reverse:
        xbuf[0:t, :] = x_ref[0]
        xbuf[t:t + halo, :] = edge
        base = 0
    else:
        xbuf[0:halo, :] = edge
        xbuf[halo:halo + t, :] = x_ref[0]
        base = halo - (RG_CONV - 1)
    xall = xbuf[...]
    n = t + halo
    y = jnp.zeros((t, c), F32) + cb_ref[...]
    for k in range(RG_CONV):
        win = xall if base + k == 0 else pltpu.roll(xall, n - (base + k), 0)
        y = y + cw_ref[k:k + 1, :] * win[:t]

    log_sig = jax.nn.log_sigmoid(lam_ref[...])
    for h in range(N_GROUPS):
        cs = slice(h * GROUP, (h + 1) * GROUP)
        yh = y[:, cs]
        gates = jnp.dot(yh.astype(BF16), wg_ref[h], preferred_element_type=F32)
        r = jax.nn.sigmoid(gates[:, :GROUP] + ba_ref[:, cs])
        ig = jax.nn.sigmoid(gates[:, GROUP:] + bi_ref[:, cs])
        log_a = RG_C * r * log_sig[:, cs]
        a = jnp.exp(log_a)
        a_buf[:, cs] = a
        b_buf[:, cs] = jnp.sqrt(1.0 - a * a) * (ig * yh)

    ngroups = t // 8
    row = lax.broadcasted_iota(jnp.int32, (8, c), 0)

    def body(gidx, cr):
        gi = (ngroups - 1 - gidx) if reverse else gidx
        r0 = pl.multiple_of(gi * 8, 8)
        a = a_buf[pl.ds(r0, 8), :]
        bb = b_buf[pl.ds(r0, 8), :]
        for k in (1, 2, 4):
            if reverse:
                a_s = pltpu.roll(a, 8 - k, 0)
                b_s = pltpu.roll(bb, 8 - k, 0)
                valid = row < 8 - k
            else:
                a_s = pltpu.roll(a, k, 0)
                b_s = pltpu.roll(bb, k, 0)
                valid = row >= k
            bb = jnp.where(valid, a * b_s + bb, bb)
            a = jnp.where(valid, a * a_s, a)
        hh = bb + a * cr
        h_ref[0, pl.ds(r0, 8), :] = hh
        edge_row = hh[0:1, :] if reverse else hh[7:8, :]
        return jnp.broadcast_to(edge_row, (8, c))

    cr = lax.fori_loop(0, ngroups, body, carry[...], unroll=4)
    carry[...] = cr
    hl_ref[0] = cr


def _rglru(p, col, h0, conv_w, conv_b, wa, ba, wi, bi, lam, reverse):
    b, l, _ = p.shape
    c = conv_w.shape[1]
    t = _pick(l, (1024, 512, 256, 128))
    nt = l // t
    hb = t // LRU_HALO
    nhb = l // LRU_HALO
    if reverse:
        tile = lambda bi_, i: (bi_, nt - 1 - i, col)
        halo = lambda bi_, i: (bi_, jnp.minimum((nt - i) * hb, nhb - 1), col)
        otile = lambda bi_, i: (bi_, nt - 1 - i, 0)
    else:
        tile = lambda bi_, i: (bi_, i, col)
        halo = lambda bi_, i: (bi_, jnp.maximum(i * hb - 1, 0), col)
        otile = lambda bi_, i: (bi_, i, 0)
    wg = jnp.concatenate([wa, wi], axis=-1).astype(BF16)
    vec = pl.BlockSpec((1, c), lambda bi_, i: (0, 0))
    state = pl.BlockSpec((1, 8, c), lambda bi_, i: (bi_, 0, 0))
    return pl.pallas_call(
        functools.partial(_lru_kernel, t=t, reverse=reverse),
        grid=(b, nt),
        in_specs=[pl.BlockSpec((1, t, c), tile),
                  pl.BlockSpec((1, LRU_HALO, c), halo),
                  pl.BlockSpec((RG_CONV, c), lambda bi_, i: (0, 0)), vec,
                  pl.BlockSpec((N_GROUPS, GROUP, 2 * GROUP), lambda bi_, i: (0, 0, 0)),
                  vec, vec, vec, state],
        out_specs=[pl.BlockSpec((1, t, c), otile), state],
        out_shape=[jax.ShapeDtypeStruct((b, l, c), F32), jax.ShapeDtypeStruct((b, 8, c), F32)],
        scratch_shapes=[pltpu.VMEM((t + LRU_HALO, c), F32), pltpu.VMEM((t, c), F32),
                        pltpu.VMEM((t, c), F32), pltpu.VMEM((8, c), F32)],
        name="rglru_rev" if reverse else "rglru_fwd",
        compiler_params=_params(("arbitrary", "arbitrary")),
    )(p, p, conv_w, conv_b.reshape(1, c), wg, ba.reshape(1, c), bi.reshape(1, c), lam.reshape(1, c), h0)


def _mixout_ab_kernel(cv_ref, hf_ref, hr_ref, pg_ref, x_ref, m_ref, g_ref, wa_ref, wb_ref, o_ref):
    rec = (hf_ref[0] + hr_ref[0]) * jax.nn.gelu(pg_ref[0])
    y = jnp.dot(cv_ref[0], wa_ref[...], preferred_element_type=F32)
    y = y + jnp.dot(rec.astype(BF16), wb_ref[...], preferred_element_type=F32)
    o_ref[0] = x_ref[0] + m_ref[0, 2:3, :] * _rms(y, g_ref[...])


def _mixout_ab(conv_out, h_f, h_r, p, x, mod, g, w_out):
    b, l, d = x.shape
    c = conv_out.shape[-1]
    tm = _pick(l, (512, 256, 128))
    half = lambda: pl.BlockSpec((1, tm, c), lambda bi, i: (bi, i, 0))
    wspec = lambda k: pl.BlockSpec((c, d), lambda bi, i: (k, 0))
    return pl.pallas_call(
        _mixout_ab_kernel,
        grid=(b, l // tm),
        in_specs=[half(), half(), half(),
                  pl.BlockSpec((1, tm, c), lambda bi, i: (bi, i, 3)),
                  pl.BlockSpec((1, tm, d), lambda bi, i: (bi, i, 0)),
                  pl.BlockSpec((1, 6, d), lambda bi, i: (bi, 0, 0)),
                  pl.BlockSpec((1, d), lambda bi, i: (0, 0)),
                  wspec(0), wspec(1)],
        out_specs=pl.BlockSpec((1, tm, d), lambda bi, i: (bi, i, 0)),
        out_shape=jax.ShapeDtypeStruct((b, l, d), F32),
        name="mixout_ab",
        compiler_params=_params(("arbitrary", "arbitrary")),
    )(conv_out, h_f, h_r, p, x, mod, g.reshape(1, d), w_out, w_out)


def _mixout_cd_kernel(f_ref, o_att_ref, x_ref, m_ref, g_ref, wa_ref, wb_ref, o_ref):
    y = jnp.dot(f_ref[0].astype(BF16), wa_ref[...], preferred_element_type=F32)
    y = y + jnp.dot(o_att_ref[0], wb_ref[...], preferred_element_type=F32)
    o_ref[0] = x_ref[0] + m_ref[0, 2:3, :] * _rms(y, g_ref[...])


def _mixout_cd(four, att, x, mod, g, w_out):
    b, l, d = x.shape
    c = four.shape[-1]
    tm = _pick(l, (512, 256, 128))
    half = lambda: pl.BlockSpec((1, tm, c), lambda bi, i: (bi, i, 0))
    wspec = lambda k: pl.BlockSpec((c, d), lambda bi, i: (k, 0))
    return pl.pallas_call(
        _mixout_cd_kernel,
        grid=(b, l // tm),
        in_specs=[half(), half(),
                  pl.BlockSpec((1, tm, d), lambda bi, i: (bi, i, 0)),
                  pl.BlockSpec((1, 6, d), lambda bi, i: (bi, 0, 0)),
                  pl.BlockSpec((1, d), lambda bi, i: (0, 0)),
                  wspec(0), wspec(1)],
        out_specs=pl.BlockSpec((1, tm, d), lambda bi, i: (bi, i, 0)),
        out_shape=jax.ShapeDtypeStruct((b, l, d), F32),
        name="mixout_cd",
        compiler_params=_params(("arbitrary", "arbitrary")),
    )(four, att, x, mod, g.reshape(1, d), w_out, w_out)


def _mlp_kernel(x_ref, m_ref, gpre_ref, gpost_ref, w1_ref, w2_ref, o_ref, h_ref, acc_ref):
    j = pl.program_id(2)
    last = pl.num_programs(2) - 1

    def chunk(h):
        u = jnp.dot(h, w1_ref[...], preferred_element_type=F32)
        u = jnp.square(jnp.maximum(u, 0.0))
        return jnp.dot(u.astype(BF16), w2_ref[...], preferred_element_type=F32)

    @pl.when(j == 0)
    def _():
        gain = gpre_ref[...] * (1.0 + m_ref[0, 4:5, :])
        x = x_ref[0]
        h = x * lax.rsqrt(jnp.mean(x * x, axis=-1, keepdims=True) + EPS) * gain + m_ref[0, 3:4, :]
        h = h.astype(BF16)
        h_ref[...] = h
        acc_ref[...] = chunk(h)

    @pl.when((j > 0) & (j < last))
    def _():
        acc_ref[...] += chunk(h_ref[...])

    @pl.when(j == last)
    def _():
        y = acc_ref[...] + chunk(h_ref[...])
        gain = m_ref[0, 5:6, :] * gpost_ref[...]
        o_ref[0] = x_ref[0] + y * lax.rsqrt(jnp.mean(y * y, axis=-1, keepdims=True) + EPS) * gain


def _mlp(x, mod, g_pre, g_post, w1, w2, layer):
    b, l, d = x.shape
    f = w1.shape[2]
    tm = _pick(l, (512, 256, 128))
    tf = _pick(f, (1024, 512, 256, 128))
    assert f // tf >= 2
    return pl.pallas_call(
        _mlp_kernel,
        grid=(b, l // tm, f // tf),
        in_specs=[pl.BlockSpec((1, tm, d), lambda bi, i, j: (bi, i, 0)),
                  pl.BlockSpec((1, 6, d), lambda bi, i, j: (bi, 0, 0)),
                  pl.BlockSpec((1, d), lambda bi, i, j: (0, 0)),
                  pl.BlockSpec((1, d), lambda bi, i, j: (0, 0)),
                  pl.BlockSpec((None, d, tf), lambda bi, i, j: (layer, 0, j)),
                  pl.BlockSpec((None, tf, d), lambda bi, i, j: (layer, j, 0))],
        out_specs=pl.BlockSpec((1, tm, d), lambda bi, i, j: (bi, i, 0)),
        out_shape=jax.ShapeDtypeStruct((b, l, d), F32),
        scratch_shapes=[pltpu.VMEM((tm, d), BF16), pltpu.VMEM((tm, d), F32)],
        name="mlp",
        compiler_params=_params(("arbitrary", "arbitrary", "arbitrary")),
    )(x, mod, g_pre.reshape(1, d), g_post.reshape(1, d), w1, w2)


def _fourier_tables(l):
    n1 = 1 << (int(math.log2(l)) // 2)
    while l % n1:
        n1 //= 2
    n2 = l // n1
    def cs(n):
        idx = np.arange(n)
        ang = 2.0 * np.pi * ((idx[:, None] * idx[None, :]) % n) / n
        return np.cos(ang), np.sin(ang)
    c1, s1 = cs(n1)
    c2, s2 = cs(n2)
    cc, sc = cs(GROUP)
    delta = 2.0 * np.pi * np.arange(n1)[:, None] / l * np.ones((1, GROUP))
    norm = 1.0 / math.sqrt(l * GROUP)
    chan = np.concatenate([cc, -sc], axis=1)
    f1 = np.concatenate([c1, s1], axis=0)
    f2 = np.concatenate([c2, s2], axis=1) * norm
    return (n1, n2, jnp.asarray(chan, BF16), jnp.asarray(f1, BF16), jnp.asarray(f2, BF16),
            jnp.asarray(np.cos(delta), F32), jnp.asarray(np.sin(delta), F32))


def _fourier_kernel(x_ref, chan_ref, f1_ref, f2_ref, dc_ref, ds_ref, o_ref, are_buf, aim_buf, tw_c, tw_s,
                    *, n1, n2, pitch):
    tw_c[...] = jnp.ones_like(tw_c)
    tw_s[...] = jnp.zeros_like(tw_s)

    def stage1(jg, _):
        j0 = jg * FFT_GROUP
        xs = jnp.concatenate([x_ref[0, pl.ds(j0 + u, n1, stride=n2), :] for u in range(FFT_GROUP)], axis=0)
        pq = jnp.dot(xs.astype(BF16), chan_ref[...], preferred_element_type=F32)
        pq = jnp.concatenate([pq[u * n1:(u + 1) * n1] for u in range(FFT_GROUP)], axis=1).astype(BF16)
        r = jnp.dot(f1_ref[...], pq, preferred_element_type=F32)
        for u in range(FFT_GROUP):
            ru = r[:, u * 2 * GROUP:(u + 1) * 2 * GROUP]
            a_re = ru[:n1, :GROUP] + ru[n1:, GROUP:]
            a_im = ru[:n1, GROUP:] - ru[n1:, :GROUP]
            tc = tw_c[...]
            ts = tw_s[...]
            row0 = pl.multiple_of((j0 + u) * pitch, 8)
            are_buf[pl.ds(row0, n1), :] = a_re * tc + a_im * ts
            aim_buf[pl.ds(row0, n1), :] = a_im * tc - a_re * ts
            tw_c[...] = tc * dc_ref[...] - ts * ds_ref[...]
            tw_s[...] = ts * dc_ref[...] + tc * ds_ref[...]
        return 0

    lax.fori_loop(0, n2 // FFT_GROUP, stage1, 0)

    def stage2(kg, _):
        k0 = kg * FFT_GROUP
        rhs = jnp.concatenate(
            [jnp.concatenate([are_buf[pl.ds(k0 + u, n2, stride=pitch), :], aim_buf[pl.ds(k0 + u, n2, stride=pitch), :]],
                             axis=0) for u in range(FFT_GROUP)], axis=1).astype(BF16)
        y = jnp.dot(f2_ref[...], rhs, preferred_element_type=F32)
        for u in range(FFT_GROUP):
            are_buf[pl.ds(k0 + u, n2, stride=pitch), :] = y[:, u * GROUP:(u + 1) * GROUP]
        return 0

    lax.fori_loop(0, n1 // FFT_GROUP, stage2, 0)

    def unpad(k2, _):
        src = pl.multiple_of(k2 * pitch, 8)
        dst = pl.multiple_of(k2 * n1, 8)
        o_ref[0, pl.ds(dst, n1), :] = are_buf[pl.ds(src, n1), :]
        return 0

    lax.fori_loop(0, n2, unpad, 0, unroll=8)


def _fourier_mix(p, l):
    b = p.shape[0]
    n1, n2, chan, f1, f2, dc, ds = _fourier_tables(l)
    const = lambda a: pl.BlockSpec(a.shape, lambda bi, g: (0, 0))
    pitch = n1 + FFT_PITCH_PAD
    return pl.pallas_call(
        functools.partial(_fourier_kernel, n1=n1, n2=n2, pitch=pitch),
        grid=(b, N_GROUPS),
        in_specs=[pl.BlockSpec((1, l, GROUP), lambda bi, g: (bi, 0, g)),
                  const(chan), const(f1), const(f2), const(dc), const(ds)],
        out_specs=pl.BlockSpec((1, l, GROUP), lambda bi, g: (bi, 0, g)),
        out_shape=jax.ShapeDtypeStruct((b, l, N_GROUPS * GROUP), F32),
        scratch_shapes=[pltpu.VMEM((n2 * pitch, GROUP), F32), pltpu.VMEM((n2 * pitch, GROUP), F32),
                        pltpu.VMEM((n1, GROUP), F32), pltpu.VMEM((n1, GROUP), F32)],
        name="fourier_mix",
        compiler_params=_params(("arbitrary", "arbitrary"), 62 * 1024 * 1024),
    )(p, chan, f1, f2, dc, ds)


def _rope_tables(l, n_ctx):
    t = np.arange(l)
    inv = ROPE_BASE ** (-np.arange(QK_ROPE // 4, dtype=np.float64) / (QK_ROPE // 4))
    ang = np.concatenate([(t // GRID_W)[:, None] * inv, (t % GRID_W)[:, None] * inv], axis=-1)
    ang = np.concatenate([ang, np.zeros((n_ctx, QK_ROPE // 2))], axis=0)
    cos, sin = np.cos(ang), np.sin(ang)
    return jnp.asarray(np.concatenate([cos, cos, -sin, sin], axis=-1), F32)


def _rope_apply(r, cs):
    tt = r * cs
    ro = tt + pltpu.roll(tt, QK_ROPE, 1)
    lane = lax.broadcasted_iota(jnp.int32, ro.shape, 1)
    return jnp.where(lane < QK_ROPE, ro, 0.0)


def _q_kernel(cq_ref, g_ref, w_ref, cs_ref, o_ref):
    n = _rms(cq_ref[0], g_ref[...]).astype(BF16)
    cs = cs_ref[...]
    for h in range(N_GROUPS):
        res = jnp.dot(n, w_ref[h], preferred_element_type=F32)
        q = jnp.concatenate([res[:, :QK_NOPE], _rope_apply(res[:, QK_NOPE:], cs)], axis=1)
        q = q * (math.log2(math.e) * QK_DIM ** -0.5)
        o_ref[0, h] = q.T.astype(o_ref.dtype)


def _mla_queries(p, l, col, q_g, w_uq, cs):
    b = p.shape[0]
    rank = w_uq.shape[0]
    w = w_uq.reshape(rank, N_GROUPS, QK_DIM)
    x1 = w[:, :, QK_NOPE:QK_NOPE + QK_ROPE // 2]
    x2 = w[:, :, QK_NOPE + QK_ROPE // 2:]
    w = jnp.concatenate([w[:, :, :QK_NOPE], x1, x2, x2, x1], axis=-1)
    w = jnp.transpose(w, (1, 0, 2)).astype(BF16)
    tm = _pick(l, (1024, 512, 256, 128))
    return pl.pallas_call(
        _q_kernel,
        grid=(b, l // tm),
        in_specs=[pl.BlockSpec((1, tm, rank), lambda bi, i: (bi, i, col)),
                  pl.BlockSpec((1, rank), lambda bi, i: (0, 0)),
                  pl.BlockSpec((N_GROUPS, rank, QK_PAD), lambda bi, i: (0, 0, 0)),
                  pl.BlockSpec((tm, 128), lambda bi, i: (i, 0))],
        out_specs=pl.BlockSpec((1, N_GROUPS, QK_PAD, tm), lambda bi, i: (bi, 0, 0, i)),
        out_shape=jax.ShapeDtypeStruct((b, N_GROUPS, QK_PAD, l), BF16),
        name="mla_q",
        compiler_params=_params(("arbitrary", "arbitrary")),
    )(p, q_g.reshape(1, rank), w, cs)


def _kv_kernel(ckv_ref, kr_ref, g_ref, w_ref, cs_ref, k_ref, v_ref):
    n = _rms(ckv_ref[0], g_ref[...]).astype(BF16)
    rope = _rope_apply(kr_ref[0], cs_ref[...]).astype(BF16)
    hw = QK_NOPE + V_DIM
    for h in range(N_GROUPS):
        res = jnp.dot(n, w_ref[:, h * hw:(h + 1) * hw], preferred_element_type=F32)
        k_ref[0, h] = jnp.concatenate([res[:, :QK_NOPE].astype(BF16), rope], axis=1)
        vt = res[:, QK_NOPE:].T.astype(BF16)
        for c in range(vt.shape[1] // 128):
            v_ref[0, h, c] = vt[:, c * 128:(c + 1) * 128]


def _mla_keys_values(p, col_kv, col_kr, kv_g, w_ukv, cs):
    b, lk, _ = p.shape
    rank = w_ukv.shape[0]
    tm = _pick(lk, (1280, 640, 256, 128))
    return pl.pallas_call(
        _kv_kernel,
        grid=(b, lk // tm),
        in_specs=[pl.BlockSpec((1, tm, rank), lambda bi, i: (bi, i, col_kv)),
                  pl.BlockSpec((1, tm, 128), lambda bi, i: (bi, i, col_kr)),
                  pl.BlockSpec((1, rank), lambda bi, i: (0, 0)),
                  pl.BlockSpec(w_ukv.shape, lambda bi, i: (0, 0)),
                  pl.BlockSpec((tm, 128), lambda bi, i: (i, 0))],
        out_specs=[pl.BlockSpec((1, N_GROUPS, tm, QK_PAD), lambda bi, i: (bi, 0, i, 0)),
                   pl.BlockSpec((1, N_GROUPS, tm // 128, V_DIM, 128), lambda bi, i: (bi, 0, i, 0, 0))],
        out_shape=[jax.ShapeDtypeStruct((b, N_GROUPS, lk, QK_PAD), BF16),
                   jax.ShapeDtypeStruct((b, N_GROUPS, lk // 128, V_DIM, 128), BF16)],
        name="mla_kv",
        compiler_params=_params(("arbitrary", "arbitrary")),
    )(p, p, kv_g.reshape(1, rank), w_ukv.astype(BF16), cs)


def _attn_kernel(qt_ref, k_ref, vt_ref, o_ref, m_ref, l_ref, acc_ref, sa_ref, sb_ref, *, tks, n_sub, rem):
    qt = qt_ref[0, 0]

    def scores(r0, width):
        return jnp.dot(k_ref[0, 0, pl.ds(r0, width), :], qt, preferred_element_type=F32)

    def update(s, c0, width):
        m_prev = m_ref[...]
        m_next = jnp.maximum(m_prev, jnp.max(s, axis=0, keepdims=True))
        p = jnp.exp2(s - pltpu.repeat(m_next, width // 8, axis=0))
        alpha = jnp.exp2(m_prev - m_next)
        l_ref[...] = alpha * l_ref[...] + jnp.sum(p, axis=0, keepdims=True)
        vt = jnp.concatenate([vt_ref[0, 0, c0 + c] for c in range(width // 128)], axis=1)
        pv = jnp.dot(vt, p.astype(BF16), preferred_element_type=F32)
        acc_ref[...] = pltpu.repeat(alpha, V_DIM // 8, axis=0) * acc_ref[...] + pv
        m_ref[...] = m_next

    def start(j):
        return pl.multiple_of(j * tks, tks)

    cpt = tks // 128
    m_ref[...] = jnp.full_like(m_ref, -jnp.inf)
    l_ref[...] = jnp.zeros_like(l_ref)
    acc_ref[...] = jnp.zeros_like(acc_ref)
    sa_ref[...] = scores(0, tks)

    def pair(i, _):
        sb_ref[...] = scores(start(2 * i + 1), tks)
        update(sa_ref[...], 2 * i * cpt, tks)
        sa_ref[...] = scores(start(2 * i + 2), tks)
        update(sb_ref[...], (2 * i + 1) * cpt, tks)
        return 0

    n_pairs = n_sub // 2
    lax.fori_loop(0, n_pairs - 1, pair, 0, unroll=5 if (n_pairs - 1) % 5 == 0 else 1)
    done = 2 * (n_pairs - 1)
    sb_ref[...] = scores(done * tks + tks, tks)
    update(sa_ref[...], done * cpt, tks)
    tail = [(j * tks, tks) for j in range(done + 2, n_sub)] + ([(n_sub * tks, rem)] if rem else [])
    bufs = [sa_ref, sb_ref]
    pending = (sb_ref, (done + 1) * tks, tks)
    for idx, (r0, width) in enumerate(tail):
        nxt = bufs[idx % 2]
        nxt[:width, :] = scores(r0, width)
        update(pending[0][:pending[2], :], pending[1] // 128, pending[2])
        pending = (nxt, r0, width)
    update(pending[0][:pending[2], :], pending[1] // 128, pending[2])
    out_t = acc_ref[...] / pltpu.repeat(l_ref[...], V_DIM // 8, axis=0)
    o_ref[0] = out_t.T.astype(o_ref.dtype)


def _attention(qt, k, vt):
    b, h, _, l = qt.shape
    lk = k.shape[2]
    tq = _pick(l, (1024, 512, 256, 128))
    tks = 512
    n_sub, rem = lk // tks, lk % tks
    assert n_sub >= 2 and rem % 128 == 0
    return pl.pallas_call(
        functools.partial(_attn_kernel, tks=tks, n_sub=n_sub, rem=rem),
        grid=(b, h, l // tq),
        in_specs=[pl.BlockSpec((1, 1, QK_PAD, tq), lambda bi, hi, i: (bi, hi, 0, i)),
                  pl.BlockSpec((1, 1, lk, QK_PAD), lambda bi, hi, i: (bi, hi, 0, 0)),
                  pl.BlockSpec((1, 1, lk // 128, V_DIM, 128), lambda bi, hi, i: (bi, hi, 0, 0, 0))],
        out_specs=pl.BlockSpec((1, tq, V_DIM), lambda bi, hi, i: (bi, i, hi)),
        out_shape=jax.ShapeDtypeStruct((b, l, h * V_DIM), BF16),
        scratch_shapes=[pltpu.VMEM((8, tq), F32), pltpu.VMEM((8, tq), F32), pltpu.VMEM((V_DIM, tq), F32),
                        pltpu.VMEM((tks, tq), F32), pltpu.VMEM((tks, tq), F32)],
        name="flash_attention",
        compiler_params=_params(("arbitrary", "arbitrary", "arbitrary")),
    )(qt, k, vt)


def _conv_lru_layer(xc, xl, ml, mc, ng, w1, w2, layer, w_in, w_out, cv_w, cv_b, cv_g, cv_beta,
                    rg_conv_w, rg_conv_b, rg_wa, rg_ba, rg_wi, rg_bi, rg_lambda, need_ctx):
    w_in = w_in.astype(BF16)
    w_out = w_out.astype(BF16)
    pc = _norm_matmul(xc, mc, ng[0], w_in, 0, 1)
    plat = _norm_matmul(xl, ml, ng[0], w_in, 0, 1)
    c = cv_w.shape[1]
    zero = jnp.zeros((xl.shape[0], 8, c), F32)
    hs_c, hs_l = [], []
    for d, reverse in enumerate((False, True)):
        args = (rg_conv_w[d], rg_conv_b[d], rg_wa[d], rg_ba[d], rg_wi[d], rg_bi[d], rg_lambda[d], reverse)
        h_c, last_c = _rglru(pc, 2, zero, *args)
        h_l, _ = _rglru(plat, 2, last_c, *args)
        hs_c.append(h_c)
        hs_l.append(h_l)

    def finish(p, hs, x, mod, shared_mod):
        conv_out = _conformer_conv(p, cv_w, cv_b, cv_g, cv_beta)
        x = _mixout_ab(conv_out, hs[0], hs[1], p, x, mod, ng[1], w_out)
        if not shared_mod:
            return _mlp(x, mod, ng[2], ng[3], w1, w2, layer)
        nb, nl, nd = x.shape
        return _mlp(x.reshape(1, nb * nl, nd), mod[:1], ng[2], ng[3], w1, w2, layer).reshape(nb, nl, nd)

    xl = finish(plat, hs_l, xl, ml, False)
    if need_ctx:
        xc = finish(pc, hs_c, xc, mc, True)
    return xc, xl


def _fourier_mla_layer(xc, xl, ml, mc, ng, w1, w2, layer, w_in, w_out, q_g, kv_g, w_uq, w_ukv):
    b, l, d = xl.shape
    n_ctx = xc.shape[1]
    c = N_GROUPS * GROUP
    q_rank = q_g.shape[0]
    kv_rank = kv_g.shape[0]
    kr = w_in[:, c + q_rank + kv_rank:]
    kr1, kr2 = kr[:, :QK_ROPE // 2], kr[:, QK_ROPE // 2:]
    n_used = c + q_rank + kv_rank + 2 * QK_ROPE
    n_pad = -n_used % 256
    w_in_p = jnp.concatenate([w_in[:, :c + q_rank + kv_rank], kr1, kr2, kr2, kr1,
                              jnp.zeros((d, n_pad), w_in.dtype)], axis=1).astype(BF16)
    lk = l + n_ctx
    p = _norm_matmul(xl, ml, ng[0], w_in_p, 0, 1, out_rows=lk)
    tmc = _pick(n_ctx, (512, 256, 128))
    p = _norm_matmul(xc, mc, ng[0], w_in_p, 0, 1, out_rows=lk, row_block_off=l // tmc, into=p)
    cs = _rope_tables(l, n_ctx)
    four = _fourier_mix(p, l)
    q = _mla_queries(p, l, c // q_rank, q_g, w_uq, cs)
    k, v = _mla_keys_values(p, (c + q_rank) // kv_rank, (c + q_rank + kv_rank) // 128, kv_g, w_ukv, cs)
    att = _attention(q, k, v)
    xl = _mixout_cd(four, att, xl, ml, ng[1], w_out.astype(BF16))
    return _mlp(xl, ml, ng[2], ng[3], w1, w2, layer)


def kernel(x, c, ctx, c_ctx, mod_w, mod_b, norm_g, mlp_w1, mlp_w2, ab_w_in, ab_w_out, cv_w, cv_b, cv_norm_g, cv_norm_b, rg_conv_w, rg_conv_b, rg_wa, rg_ba, rg_wi, rg_bi, rg_lambda, cd_w_in, cd_w_out, mla_q_norm_g, mla_kv_norm_g, mla_w_uq, mla_w_ukv):
    b, l, d = x.shape
    depth = mod_w.shape[0]
    assert b + 1 <= 8 and l % GRID_W == 0
    cond8 = jnp.concatenate([c, c_ctx[None, :], jnp.zeros((8 - b - 1, d), F32)], axis=0)
    mods = _modulation(cond8, mod_w, mod_b)
    w1, w2 = mlp_w1.astype(BF16), mlp_w2.astype(BF16)
    xc, xl = ctx, x
    for i in range(depth):
        need_ctx = i < depth - 1
        ml = mods[i, :b].reshape(b, 6, d)
        mc = jnp.broadcast_to(mods[i, b].reshape(1, 6, d), (b, 6, d))
        j = i // 2
        if i % 2 == 0:
            xc, xl = _conv_lru_layer(xc, xl, ml, mc, norm_g[i], w1, w2, i, ab_w_in[j], ab_w_out[j],
                                     cv_w[j], cv_b[j], cv_norm_g[j], cv_norm_b[j], rg_conv_w[j], rg_conv_b[j],
                                     rg_wa[j], rg_ba[j], rg_wi[j], rg_bi[j], rg_lambda[j], need_ctx)
        else:
            xl = _fourier_mla_layer(xc, xl, ml, mc, norm_g[i], w1, w2, i, cd_w_in[j], cd_w_out[j],
                                    mla_q_norm_g[j], mla_kv_norm_g[j], mla_w_uq[j], mla_w_ukv[j])
    return xl
```
